```python
import jax, jax.numpy as jnp
from jax import lax
import numpy as np

D_MODEL = 2048
BATCH = 8
SEQ = 4096
DEPTH = 2
DEC_BATCH = 2
DEC_SEQ = 4096
PAST_LEN = 128

HA = 8
HKV = 2
GROUP = HA // HKV
DH = 128
WA = HA * DH
WKV = HKV * DH
WINDOW = 128
BLOCK = 128
N_BUCKETS = 32
MAX_DIST = 128
HB = 4
DK = 128
DV = 256
WBK = HB * DK
WB = HB * DV
GATE_RANK = 16
GATE_NORM = 16.0
CHUNK = 64
SPLITS = [WA, WKV, WKV, WA, WBK, WBK, WB, WB, GATE_RANK, GATE_RANK]
D_IN = sum(SPLITS)
D_MIX = WA + WB
NEG = -1e30

kernel_name = "hymba_style_bidir_swa_gla_encoder"


def rms_norm(x, w, eps=1e-6):
    xf = x.astype(jnp.float32)
    y = xf * lax.rsqrt(jnp.mean(xf * xf, axis=-1, keepdims=True) + eps)
    return (y * w.astype(jnp.float32)).astype(x.dtype)


def _band_static():
    qi = np.arange(BLOCK)[:, None]
    kj = np.arange(3 * BLOCK)[None, :]
    rel = kj - BLOCK - qi
    nb = N_BUCKETS // 2
    max_exact = nb // 2
    n = np.abs(rel)
    large = max_exact + (np.log(np.maximum(n, 1) / max_exact) / np.log(MAX_DIST / max_exact)
                         * (nb - max_exact)).astype(np.int32)
    large = np.minimum(large, nb - 1)
    bucket = (rel > 0).astype(np.int32) * nb + np.where(n < max_exact, n, large)
    return bucket.astype(np.int32), (n <= WINDOW)


def windowed_gqa(q, k, v, rel_bias, sink):
    B, L = q.shape[0], q.shape[1]
    nb = L // BLOCK
    bucket, in_band = _band_static()
    pos_bias = jnp.transpose(rel_bias[bucket], (2, 0, 1)).astype(jnp.float32)
    pad = ((0, 0), (BLOCK, BLOCK), (0, 0), (0, 0))
    kp = jnp.pad(k, pad).reshape(B, nb + 2, BLOCK, HKV, DH)
    vp = jnp.pad(v, pad).reshape(B, nb + 2, BLOCK, HKV, DH)
    kw = jnp.concatenate([kp[:, :-2], kp[:, 1:-1], kp[:, 2:]], axis=2)
    vw = jnp.concatenate([vp[:, :-2], vp[:, 1:-1], vp[:, 2:]], axis=2)
    qb = q.reshape(B, nb, BLOCK, HKV, GROUP, DH)
    s = jnp.einsum('bnqhgd,bnkhd->bnhgqk', qb, kw).astype(jnp.float32) * (DH ** -0.5)
    s = s.reshape(B, nb, HA, BLOCK, 3 * BLOCK) + pos_bias
    key_pos = np.arange(nb)[:, None] * BLOCK + np.arange(3 * BLOCK)[None, :] - BLOCK
    valid = in_band[None] & ((key_pos >= 0) & (key_pos < L))[:, None, :]
    s = jnp.where(valid[None, :, None], s, NEG)
    sink_f = sink.astype(jnp.float32)[None, None, :, None, None]
    m = jnp.maximum(jnp.max(s, axis=-1, keepdims=True), sink_f)
    p = jnp.exp(s - m)
    p = p / (jnp.sum(p, axis=-1, keepdims=True) + jnp.exp(sink_f - m))
    p = p.astype(v.dtype).reshape(B, nb, HKV, GROUP, BLOCK, 3 * BLOCK)
    o = jnp.einsum('bnhgqk,bnkhd->bnqhgd', p, vw)
    return o.reshape(B, L, WA)


def gla_direction(q, k, v, g):
    B, L = q.shape[0], q.shape[1]
    nc = L // CHUNK
    q = q.reshape(B, nc, CHUNK, HB, DK)
    k = k.reshape(B, nc, CHUNK, HB, DK)
    v = v.reshape(B, nc, CHUNK, HB, DV)
    b = jnp.cumsum(g.reshape(B, nc, CHUNK, HB, DK), axis=2)
    b_last = b[:, :, -1:]
    q_dec = q * jnp.exp(b)
    k_dec = k * jnp.exp(-b)
    k_tail = k * jnp.exp(b_last - b)
    tri = np.tril(np.ones((CHUNK, CHUNK), dtype=bool))
    a = jnp.where(tri, jnp.einsum('bnihd,bnjhd->bnhij', q_dec, k_dec), 0.0)
    o_intra = jnp.einsum('bnhij,bnjhv->bnihv', a, v)
    kv = jnp.einsum('bnjhd,bnjhv->bnhdv', k_tail, v)
    decay = jnp.exp(b_last[:, :, 0])

    def step(S, inp):
        qd, dec, kvc = inp
        o = jnp.einsum('bihd,bhdv->bihv', qd, S)
        return S * dec[..., None] + kvc, o

    S0 = jnp.zeros((B, HB, DK, DV), jnp.float32)
    _, o_inter = lax.scan(step, S0, (jnp.moveaxis(q_dec, 1, 0), jnp.moveaxis(decay, 1, 0),
                                     jnp.moveaxis(kv, 1, 0)))
    o = o_intra + jnp.moveaxis(o_inter, 0, 1)
    return o.reshape(B, L, HB, DV)


def hybrid_layer(x, rel_bias, w_in, w_gk_f, b_gk_f, w_gk_b, b_gk_b, sink, gla_norm, w_out,
                 norm_pre, norm_post):
    B, L = x.shape[0], x.shape[1]
    h = rms_norm(x, norm_pre)
    proj = h @ w_in
    qa, ka, va, za, qb, kb, vb, zb, lrf, lrb = jnp.split(proj, list(np.cumsum(SPLITS)[:-1]), axis=-1)
    attn = windowed_gqa(qa.reshape(B, L, HA, DH), ka.reshape(B, L, HKV, DH),
                        va.reshape(B, L, HKV, DH), rel_bias, sink)
    attn = attn * jax.nn.silu(za)
    f32 = jnp.float32
    q_b = qb.astype(f32).reshape(B, L, HB, DK) * (DK ** -0.5)
    k_b = kb.astype(f32).reshape(B, L, HB, DK)
    v_b = vb.astype(f32).reshape(B, L, HB, DV)
    g_f = (jax.nn.log_sigmoid((lrf @ w_gk_f + b_gk_f).astype(f32)) / GATE_NORM).reshape(B, L, HB, DK)
    g_b = (jax.nn.log_sigmoid((lrb @ w_gk_b + b_gk_b).astype(f32)) / GATE_NORM).reshape(B, L, HB, DK)
    o_fwd = gla_direction(q_b, k_b, v_b, g_f)
    o_bwd = jnp.flip(gla_direction(jnp.flip(q_b, 1), jnp.flip(k_b, 1), jnp.flip(v_b, 1),
                                   jnp.flip(g_b, 1)), 1)
    o_gla = rms_norm((o_fwd + o_bwd).astype(x.dtype), gla_norm)
    o_gla = o_gla.reshape(B, L, WB) * jax.nn.silu(zb)
    mix = jnp.concatenate([attn, o_gla], axis=-1) @ w_out
    return x + rms_norm(mix, norm_post)


def setup_inputs(seed: int = 0) -> dict:
    key = jax.random.key(seed)
    ks = jax.random.split(key, 16)
    nrm = jax.random.normal
    return {
        "x_prompt": nrm(ks[0], (BATCH, SEQ, D_MODEL), jnp.float32),
        "x_sample": nrm(ks[1], (DEC_BATCH, DEC_SEQ, D_MODEL), jnp.float32),
        "rel_bias": 0.5 * nrm(ks[2], (N_BUCKETS, HA), jnp.float32),
        "w_in": nrm(ks[3], (DEPTH, D_MODEL, D_IN), jnp.float32) * D_MODEL ** -0.5,
        "w_gk_fwd": nrm(ks[4], (DEPTH, GATE_RANK, WBK), jnp.float32) * GATE_RANK ** -0.5,
        "b_gk_fwd": 0.1 * nrm(ks[5], (DEPTH, WBK), jnp.float32),
        "w_gk_bwd": nrm(ks[6], (DEPTH, GATE_RANK, WBK), jnp.float32) * GATE_RANK ** -0.5,
        "b_gk_bwd": 0.1 * nrm(ks[7], (DEPTH, WBK), jnp.float32),
        "sink": 0.5 * nrm(ks[8], (DEPTH, HA), jnp.float32),
        "gla_norm": 1.0 + 0.02 * nrm(ks[9], (DEPTH, DV), jnp.float32),
        "w_out": nrm(ks[10], (DEPTH, D_MIX, D_MODEL), jnp.float32) * D_MIX ** -0.5,
        "norm_pre": 1.0 + 0.02 * nrm(ks[11], (DEPTH, D_MODEL), jnp.float32),
        "norm_post": 1.0 + 0.02 * nrm(ks[12], (DEPTH, D_MODEL), jnp.float32),
    }


def reference(x_prompt, x_sample, rel_bias, w_in, w_gk_fwd, b_gk_fwd, w_gk_bwd, b_gk_bwd, sink,
              gla_norm, w_out, norm_pre, norm_post):
    y_prompt = x_prompt
    y_sample = x_sample
    for l in range(DEPTH):
        y_prompt = hybrid_layer(y_prompt, rel_bias, w_in[l], w_gk_fwd[l], b_gk_fwd[l], w_gk_bwd[l],
                                b_gk_bwd[l], sink[l], gla_norm[l], w_out[l], norm_pre[l], norm_post[l])
        y_sample = hybrid_layer(y_sample, rel_bias, w_in[l], w_gk_fwd[l], b_gk_fwd[l], w_gk_bwd[l],
                                b_gk_bwd[l], sink[l], gla_norm[l], w_out[l], norm_pre[l], norm_post[l])
    return (y_prompt, y_sample)
```

```python
import functools

import jax
import jax.numpy as jnp
import numpy as np
from jax import lax
from jax.experimental import pallas as pl
from jax.experimental.pallas import tpu as pltpu

D_MODEL = 2048
HA, HKV, DH = 8, 2, 128
GROUP = HA // HKV
WINDOW, BLOCK = 128, 128
N_BUCKETS, MAX_DIST = 32, 128
HB, DK, DV = 4, 128, 256
GATE_RANK, GATE_NORM = 16, 16.0
CHUNK = 64
NEG = -1e30
EPS = 1e-6

LANE = 128
GROUP_ROWS = 2 * CHUNK

WA = HA * DH
WKV = HKV * DH
WBK = HB * DK
WB = HB * DV
N_MAIN = 2 * WA + 2 * WKV + 2 * WBK + 2 * WB
CB_QA = 0
CB_KA = CB_QA + WA // LANE
CB_VA = CB_KA + WKV // LANE
CB_ZA = CB_VA + WKV // LANE
CB_QB = CB_ZA + WA // LANE
CB_KB = CB_QB + WBK // LANE
CB_VB = CB_KB + WBK // LANE
CB_ZB = CB_VB + WB // LANE
N_CB = N_MAIN // LANE

VMEM_LIMIT = 56 * 1024 * 1024

_f32 = jnp.float32
_bf16 = jnp.bfloat16


def _nt(a, b):
    return lax.dot_general(a, b, (((1,), (1,)), ((), ())), preferred_element_type=_f32)


def _nn(a, b):
    return jnp.dot(a, b, preferred_element_type=_f32)


def _silu(z):
    return z * (1.0 / (1.0 + jnp.exp(-z)))


def _log_sigmoid(x):
    return jnp.minimum(x, 0.0) - jnp.log(1.0 + jnp.exp(-jnp.abs(x)))


IN_TM = 1024
IN_TN = 512
IN_RC = 128


def _in_proj_kernel(x_ref, nw_ref, w_ref, wlr_ref, o_ref, lr_ref, xn_ref):
    j = pl.program_id(2)

    @pl.when(j == 0)
    def _():
        nw = nw_ref[...]

        def body(r, c):
            rows = pl.ds(pl.multiple_of(r * IN_RC, IN_RC), IN_RC)
            x = x_ref[0, rows, :]
            ms = jnp.mean(x * x, axis=-1, keepdims=True)
            xn = (x * lax.rsqrt(ms + EPS) * nw).astype(_bf16)
            xn_ref[rows, :] = xn
            lr_ref[0, rows, :] = _nn(xn, wlr_ref[...])[:, : 2 * GATE_RANK]
            return c

        lax.fori_loop(0, IN_TM // IN_RC, body, 0)

    acc = _nn(xn_ref[...], w_ref[...])
    for c in range(IN_TN // LANE):
        o_ref[0, c] = acc[:, c * LANE:(c + 1) * LANE].astype(_bf16)


def _in_proj(x, norm_w, w_main, w_lr):
    B, L, _ = x.shape
    grid = (B, L // IN_TM, N_MAIN // IN_TN)
    return pl.pallas_call(
        _in_proj_kernel,
        grid=grid,
        in_specs=[
            pl.BlockSpec((1, IN_TM, D_MODEL), lambda b, i, j: (b, i, 0)),
            pl.BlockSpec((1, D_MODEL), lambda b, i, j: (0, 0)),
            pl.BlockSpec((D_MODEL, IN_TN), lambda b, i, j: (0, j)),
            pl.BlockSpec((D_MODEL, LANE), lambda b, i, j: (0, 0)),
        ],
        out_specs=[
            pl.BlockSpec((1, IN_TN // LANE, IN_TM, LANE), lambda b, i, j: (b, j, i, 0)),
            pl.BlockSpec((1, IN_TM, 2 * GATE_RANK), lambda b, i, j: (b, i, 0)),
        ],
        out_shape=[
            jax.ShapeDtypeStruct((B, N_CB, L, LANE), _bf16),
            jax.ShapeDtypeStruct((B, L, 2 * GATE_RANK), _f32),
        ],
        scratch_shapes=[pltpu.VMEM((IN_TM, D_MODEL), _bf16)],
        compiler_params=pltpu.CompilerParams(
            dimension_semantics=("parallel", "parallel", "arbitrary"),
            vmem_limit_bytes=VMEM_LIMIT),
        name="in_proj",
    )(x, norm_w, w_main, w_lr)


AT_TQ = 1024


def _attn_kernel(sink_ref, q_ref, k_ref, v_ref, z_ref, bias_ref, o_ref, *, nblk):
    h = pl.program_id(1)
    i = pl.program_id(2)
    nsub = AT_TQ // BLOCK
    scale = DH ** -0.5
    col = lax.broadcasted_iota(jnp.int32, (1, 3 * BLOCK), 1)

    def sub(s, c):
        blk = i * nsub + s
        rows = pl.ds(pl.multiple_of(s * BLOCK, BLOCK), BLOCK)

        def kv_rows(j):
            return pl.ds(pl.multiple_of(j * BLOCK, BLOCK), BLOCK)

        lb = jnp.maximum(blk - 1, 0)
        rb = jnp.minimum(blk + 1, nblk - 1)
        kw = jnp.concatenate([k_ref[0, 0, kv_rows(lb), :], k_ref[0, 0, kv_rows(blk), :],
                              k_ref[0, 0, kv_rows(rb), :]], axis=0)
        vw = jnp.concatenate([v_ref[0, 0, kv_rows(lb), :], v_ref[0, 0, kv_rows(blk), :],
                              v_ref[0, 0, kv_rows(rb), :]], axis=0)
        q4 = q_ref[0, :, rows, :].reshape(GROUP * BLOCK, DH)
        s4 = _nt(q4, kw) * scale
        bad = ((col < BLOCK) & (blk == 0)) | ((col >= 2 * BLOCK) & (blk == nblk - 1))
        ps, invs = [], []
        for g in range(GROUP):
            sg = s4[g * BLOCK:(g + 1) * BLOCK] + bias_ref[g]
            sg = jnp.where(bad, NEG, sg)
            sk = sink_ref[h * GROUP + g]
            m = jnp.maximum(jnp.max(sg, axis=-1, keepdims=True), sk)
            p = jnp.exp(sg - m)
            den = jnp.sum(p, axis=-1, keepdims=True) + jnp.exp(sk - m)
            ps.append(p.astype(_bf16))
            invs.append(1.0 / den)
        o4 = _nn(jnp.concatenate(ps, axis=0), vw)
        for g in range(GROUP):
            z = z_ref[0, g, rows, :].astype(_f32)
            o_ref[0, g, rows, :] = (o4[g * BLOCK:(g + 1) * BLOCK] * invs[g] * _silu(z)).astype(_bf16)
        return c

    lax.fori_loop(0, nsub, sub, 0)


def _attention(proj, pos_bias, sink):
    B, _, L, _ = proj.shape
    nblk = L // BLOCK
    tq = min(AT_TQ, L)
    assert tq == AT_TQ
    grid = (B, HKV, L // AT_TQ)
    return pl.pallas_call(
        functools.partial(_attn_kernel, nblk=nblk),
        grid=grid,
        in_specs=[
            pl.BlockSpec(memory_space=pltpu.SMEM),
            pl.BlockSpec((1, GROUP, AT_TQ, LANE), lambda b, h, i: (b, CB_QA // GROUP + h, i, 0)),
            pl.BlockSpec((1, 1, L, LANE), lambda b, h, i: (b, CB_KA + h, 0, 0)),
            pl.BlockSpec((1, 1, L, LANE), lambda b, h, i: (b, CB_VA + h, 0, 0)),
            pl.BlockSpec((1, GROUP, AT_TQ, LANE), lambda b, h, i: (b, CB_ZA // GROUP + h, i, 0)),
            pl.BlockSpec((GROUP, BLOCK, 3 * BLOCK), lambda b, h, i: (h, 0, 0)),
        ],
        out_specs=pl.BlockSpec((1, GROUP, AT_TQ, LANE), lambda b, h, i: (b, h, i, 0)),
        out_shape=jax.ShapeDtypeStruct((B, HA, L, LANE), _bf16),
        compiler_params=pltpu.CompilerParams(
            dimension_semantics=("parallel", "parallel", "parallel"),
            vmem_limit_bytes=VMEM_LIMIT),
        name="attention",
    )(sink, proj, proj, proj, proj, pos_bias)


def _gla_kernel(q_ref, k_ref, v_ref, z_ref, lr_ref, wgf_ref, wgb_ref, bgf_ref, bgb_ref, gn_ref,
                o_ref,
                qd_ref, kd_ref, kt_ref, qs_ref, ks_ref, gdec_ref, vt_ref, st_ref, oacc_ref, *, ngroups):
    R = GROUP_ROWS
    scale = DK ** -0.5
    row = lax.broadcasted_iota(jnp.int32, (R, R), 0)
    colm = lax.broadcasted_iota(jnp.int32, (R, R), 1)
    same_chunk = (row >= CHUNK) == (colm >= CHUNK)
    tri = jnp.where(same_chunk & (colm <= row), 1.0, 0.0).astype(_bf16)
    mask_d = (same_chunk & (colm <= row), same_chunk & (colm >= row))
    mask_o = ((row >= CHUNK) & (colm < CHUNK), (row < CHUNK) & (colm >= CHUNK))
    r1 = lax.broadcasted_iota(jnp.int32, (R, 1), 0)
    second = r1 >= CHUNK

    def grp_rows(g):
        return pl.ds(pl.multiple_of(g * R, R), R)

    def pre(t, c):
        rows = grp_rows(t)
        lr = lr_ref[0, rows, :].astype(_bf16)
        xf = _nn(lr, wgf_ref[...]) + bgf_ref[...]
        xb = _nn(lr, wgb_ref[...]) + bgb_ref[...]
        g2 = jnp.concatenate([_log_sigmoid(xf), _log_sigmoid(xb)], axis=-1) * (1.0 / GATE_NORM)
        hi = g2.astype(_bf16)
        lo = (g2 - hi.astype(_f32)).astype(_bf16)
        pre2 = _nn(tri, hi) + _nn(tri, lo)
        tot = jnp.concatenate(
            [jnp.broadcast_to(pre2[CHUNK - 1:CHUNK], (CHUNK, 2 * DK)),
             jnp.broadcast_to(pre2[R - 1:R], (CHUNK, 2 * DK))], axis=0)
        oth = jnp.concatenate([tot[CHUNK:], tot[:CHUNK]], axis=0)
        gtot = tot + oth
        suf2 = tot - pre2 + g2
        q = q_ref[0, 0, rows, :].astype(_f32) * scale
        k = k_ref[0, 0, rows, :].astype(_f32)
        for d in range(2):
            sl = slice(d * DK, (d + 1) * DK)
            b = (pre2 if d == 0 else suf2)[:, sl]
            extra = jnp.where(second if d == 0 else ~second, oth[:, sl], 0.0)
            bg = b + extra
            qd_ref[d, rows, :] = (q * jnp.exp(b)).astype(_bf16)
            kd_ref[d, rows, :] = (k * jnp.exp(-b)).astype(_bf16)
            kt_ref[d, rows, :] = (k * jnp.exp(tot[:, sl] - b)).astype(_bf16)
            qs_ref[d, rows, :] = (q * jnp.exp(bg)).astype(_bf16)
            ks_ref[d, rows, :] = (k * jnp.exp(gtot[:, sl] - bg)).astype(_bf16)
            gdec_ref[d, t] = jnp.exp(gtot[0:8, sl])
        v2 = jnp.concatenate([v_ref[0, 0, rows, :], v_ref[0, 1, rows, :]], axis=-1).astype(_f32)
        vt_ref[t] = v2.T.astype(_bf16)
        return c

    lax.fori_loop(0, ngroups, pre, 0)

    st_ref[...] = jnp.zeros_like(st_ref)
    gn = gn_ref[...]

    def dir_step(d, g):
        rows = grp_rows(g)
        qd = qd_ref[d, rows, :]
        kk = jnp.concatenate([kd_ref[d, rows, :], kt_ref[d, rows, :]], axis=0)
        p = _nt(qd, kk)
        a = jnp.where(mask_d[d], p[:, :R], jnp.where(mask_o[d], p[:, R:], 0.0)).astype(_bf16)
        v2 = jnp.concatenate([v_ref[0, 0, rows, :], v_ref[0, 1, rows, :]], axis=-1)
        st = st_ref[d]
        o = _nn(a, v2) + _nt(qs_ref[d, rows, :], st.astype(_bf16))
        st_ref[d] = st * gdec_ref[d, g][0:1] + _nn(vt_ref[g], ks_ref[d, rows, :])
        return rows, o

    def first_half(t, c):
        for d in range(2):
            rows, o = dir_step(d, t if d == 0 else ngroups - 1 - t)
            oacc_ref[rows, :] = o
        return c

    def second_half(t, c):
        for d in range(2):
            rows, o = dir_step(d, t if d == 0 else ngroups - 1 - t)
            o = o + oacc_ref[rows, :]
            ms = jnp.mean(o * o, axis=-1, keepdims=True)
            y = o * lax.rsqrt(ms + EPS) * gn
            z = jnp.concatenate([z_ref[0, 0, rows, :], z_ref[0, 1, rows, :]], axis=-1).astype(_f32)
            y = (y * _silu(z)).astype(_bf16)
            o_ref[0, 0, rows, :] = y[:, :LANE]
            o_ref[0, 1, rows, :] = y[:, LANE:]
        return c

    lax.fori_loop(0, ngroups // 2, first_half, 0)
    lax.fori_loop(ngroups // 2, ngroups, second_half, 0)


def _gla(proj, lr, wgf, wgb, bgf, bgb, gla_norm):
    B, _, L, _ = proj.shape
    ngroups = L // GROUP_ROWS
    assert ngroups % 2 == 0
    grid = (B, HB)
    return pl.pallas_call(
        functools.partial(_gla_kernel, ngroups=ngroups),
        grid=grid,
        in_specs=[
            pl.BlockSpec((1, 1, L, LANE), lambda b, h: (b, CB_QB + h, 0, 0)),
            pl.BlockSpec((1, 1, L, LANE), lambda b, h: (b, CB_KB + h, 0, 0)),
            pl.BlockSpec((1, 2, L, LANE), lambda b, h: (b, CB_VB // 2 + h, 0, 0)),
            pl.BlockSpec((1, 2, L, LANE), lambda b, h: (b, CB_ZB // 2 + h, 0, 0)),
            pl.BlockSpec((1, L, 2 * GATE_RANK), lambda b, h: (b, 0, 0)),
            pl.BlockSpec((2 * GATE_RANK, DK), lambda b, h: (0, h)),
            pl.BlockSpec((2 * GATE_RANK, DK), lambda b, h: (0, h)),
            pl.BlockSpec((1, DK), lambda b, h: (0, h)),
            pl.BlockSpec((1, DK), lambda b, h: (0, h)),
            pl.BlockSpec((1, DV), lambda b, h: (0, 0)),
        ],
        out_specs=pl.BlockSpec((1, 2, L, LANE), lambda b, h: (b, h, 0, 0)),
        out_shape=jax.ShapeDtypeStruct((B, 2 * HB, L, LANE), _bf16),
        scratch_shapes=[
            pltpu.VMEM((2, L, DK), _bf16),
            pltpu.VMEM((2, L, DK), _bf16),
            pltpu.VMEM((2, L, DK), _bf16),
            pltpu.VMEM((2, L, DK), _bf16),
            pltpu.VMEM((2, L, DK), _bf16),
            pltpu.VMEM((2, ngroups, 8, DK), _f32),
            pltpu.VMEM((ngroups, DV, GROUP_ROWS), _bf16),
            pltpu.VMEM((2, DV, DK), _f32),
            pltpu.VMEM((L, DV), _f32),
        ],
        compiler_params=pltpu.CompilerParams(
            dimension_semantics=("parallel", "parallel"),
            vmem_limit_bytes=VMEM_LIMIT),
        name="gla",
    )(proj, proj, proj, proj, lr, wgf, wgb, bgf, bgb, gla_norm)


OUT_TM = 512


def _out_proj_kernel(a_ref, g_ref, w_ref, x_ref, nw_ref, o_ref):
    mix = jnp.concatenate([a_ref[0, c] for c in range(HA)] + [g_ref[0, c] for c in range(2 * HB)], axis=-1)
    y = _nn(mix, w_ref[...])
    ms = jnp.mean(y * y, axis=-1, keepdims=True)
    o_ref[0] = x_ref[0] + y * lax.rsqrt(ms + EPS) * nw_ref[...]


def _out_proj(attn, gla, w_out, x, norm_w):
    B, L, _ = x.shape
    grid = (B, L // OUT_TM)
    return pl.pallas_call(
        _out_proj_kernel,
        grid=grid,
        in_specs=[
            pl.BlockSpec((1, HA, OUT_TM, LANE), lambda b, i: (b, 0, i, 0)),
            pl.BlockSpec((1, 2 * HB, OUT_TM, LANE), lambda b, i: (b, 0, i, 0)),
            pl.BlockSpec((D_MODEL, D_MODEL), lambda b, i: (0, 0)),
            pl.BlockSpec((1, OUT_TM, D_MODEL), lambda b, i: (b, i, 0)),
            pl.BlockSpec((1, D_MODEL), lambda b, i: (0, 0)),
        ],
        out_specs=pl.BlockSpec((1, OUT_TM, D_MODEL), lambda b, i: (b, i, 0)),
        out_shape=jax.ShapeDtypeStruct((B, L, D_MODEL), _f32),
        compiler_params=pltpu.CompilerParams(
            dimension_semantics=("parallel", "parallel"),
            vmem_limit_bytes=VMEM_LIMIT),
        name="out_proj",
    )(attn, gla, w_out, x, norm_w)


def _band_tables():
    qi = np.arange(BLOCK)[:, None]
    kj = np.arange(3 * BLOCK)[None, :]
    rel = kj - BLOCK - qi
    nb = N_BUCKETS // 2
    max_exact = nb // 2
    n = np.abs(rel)
    large = max_exact + (np.log(np.maximum(n, 1) / max_exact) / np.log(MAX_DIST / max_exact)
                         * (nb - max_exact)).astype(np.int32)
    large = np.minimum(large, nb - 1)
    bucket = (rel > 0).astype(np.int32) * nb + np.where(n < max_exact, n, large)
    return bucket.astype(np.int32), (n <= WINDOW)


def _layer(x, pos_bias, w_main, w_lr, wgf, wgb, bgf, bgb, sink, gla_norm, w_out, norm_pre, norm_post):
    proj, lr = _in_proj(x, norm_pre, w_main, w_lr)
    attn = _attention(proj, pos_bias, sink)
    gla = _gla(proj, lr, wgf, wgb, bgf, bgb, gla_norm)
    return _out_proj(attn, gla, w_out, x, norm_post)


def _layer_params(l, w_in, w_gk_fwd, b_gk_fwd, w_gk_bwd, b_gk_bwd, sink, gla_norm, w_out, norm_pre, norm_post):
    w_main = w_in[l][:, :N_MAIN].astype(_bf16)
    w_lr = jnp.pad(w_in[l][:, N_MAIN:], ((0, 0), (0, LANE - 2 * GATE_RANK))).astype(_bf16)
    zeros = jnp.zeros((GATE_RANK, WBK), _f32)
    wgf = jnp.concatenate([w_gk_fwd[l], zeros], axis=0).astype(_bf16)
    wgb = jnp.concatenate([zeros, w_gk_bwd[l]], axis=0).astype(_bf16)
    return (w_main, w_lr, wgf, wgb, b_gk_fwd[l][None, :], b_gk_bwd[l][None, :], sink[l],
            gla_norm[l][None, :], w_out[l].astype(_bf16), norm_pre[l][None, :], norm_post[l][None, :])


def kernel(x_prompt, x_sample, rel_bias, w_in, w_gk_fwd, b_gk_fwd, w_gk_bwd, b_gk_bwd, sink, gla_norm, w_out, norm_pre, norm_post):
    bucket, in_band = _band_tables()
    pos_bias = jnp.transpose(rel_bias[bucket], (2, 0, 1)).astype(_f32)
    pos_bias = jnp.where(in_band[None], pos_bias, NEG)
    y_prompt, y_sample = x_prompt, x_sample
    for l in range(w_in.shape[0]):
        params = _layer_params(l, w_in, w_gk_fwd, b_gk_fwd, w_gk_bwd, b_gk_bwd, sink, gla_norm, w_out,
                               norm_pre, norm_post)
        y_prompt = _layer(y_prompt, pos_bias, *params)
        y_sample = _layer(y_sample, pos_bias, *params)
    return (y_prompt, y_sample)
```

```python
import functools

import jax
import jax.numpy as jnp
import numpy as np
from jax import lax
from jax.experimental import pallas as pl
from jax.experimental.pallas import tpu as pltpu

D_MODEL = 2048
HA, HKV, DH = 8, 2, 128
GROUP = HA // HKV
WINDOW, BLOCK = 128, 128
N_BUCKETS, MAX_DIST = 32, 128
HB, DK, DV = 4, 128, 256
GATE_RANK, GATE_NORM = 16, 16.0
CHUNK = 64
NEG = -1e30
EPS = 1e-6

LANE = 128
GROUP_ROWS = 2 * CHUNK

WA = HA * DH
WKV = HKV * DH
WBK = HB * DK
WB = HB * DV
N_MAIN = 2 * WA + 2 * WKV + 2 * WBK + 2 * WB
CB_QA = 0
CB_KA = CB_QA + WA // LANE
CB_VA = CB_KA + WKV // LANE
CB_ZA = CB_VA + WKV // LANE
CB_QB = CB_ZA + WA // LANE
CB_KB = CB_QB + WBK // LANE
CB_VB = CB_KB + WBK // LANE
CB_ZB = CB_VB + WB // LANE
N_CB = N_MAIN // LANE

VMEM_LIMIT = 56 * 1024 * 1024

_f32 = jnp.float32
_bf16 = jnp.bfloat16


def _nt(a, b):
    return lax.dot_general(a, b, (((1,), (1,)), ((), ())), preferred_element_type=_f32)


def _nn(a, b):
    return jnp.dot(a, b, preferred_element_type=_f32)


def _silu(z):
    return z * (1.0 / (1.0 + jnp.exp(-z)))


def _log_sigmoid(x):
    return jnp.minimum(x, 0.0) - jnp.log(1.0 + jnp.exp(-jnp.abs(x)))


IN_TM = 512
IN_TN = 512
IN_RC = 128


def _in_proj_kernel(x_ref, nw_ref, w_ref, o_ref, lr_ref, xn_ref):
    g = pl.program_id(0)

    @pl.when(g == 0)
    def _():
        xn_ref[1] = jnp.zeros((IN_TM, D_MODEL), _bf16)

    cur = g % 2
    nw = nw_ref[0]
    for r in range(IN_TM // IN_RC):
        rows = slice(r * IN_RC, (r + 1) * IN_RC)
        x = x_ref[0, rows, :]
        ms = jnp.mean(x * x, axis=-1, keepdims=True)
        xn_ref[cur, rows, :] = (x * lax.rsqrt(ms + EPS) * nw).astype(_bf16)

    xp = xn_ref[1 - cur]
    for j in range(N_MAIN // IN_TN):
        acc = _nn(xp, w_ref[0, :, j * IN_TN:(j + 1) * IN_TN])
        for c in range(IN_TN // LANE):
            o_ref[0, j * (IN_TN // LANE) + c] = acc[:, c * LANE:(c + 1) * LANE].astype(_bf16)
    lr_ref[0] = _nn(xp, w_ref[0, :, N_MAIN:N_MAIN + 2 * GATE_RANK])


def _in_proj(x, norm_w, w_in, layer):
    B, L, _ = x.shape
    nm = L // IN_TM
    steps = B * nm

    def x_map(g):
        t = jnp.minimum(g, steps - 1)
        return (t // nm, t % nm, 0)

    def o_map(g):
        t = jnp.maximum(g - 1, 0)
        return (t // nm, 0, t % nm, 0)

    def lr_map(g):
        t = jnp.maximum(g - 1, 0)
        return (t // nm, t % nm, 0)

    return pl.pallas_call(
        _in_proj_kernel,
        grid=(steps + 1,),
        in_specs=[
            pl.BlockSpec((1, IN_TM, D_MODEL), x_map),
            pl.BlockSpec((1, 1, D_MODEL), lambda g: (layer, 0, 0)),
            pl.BlockSpec((1, D_MODEL, w_in.shape[2]), lambda g: (layer, 0, 0), pipeline_mode=pl.Buffered(1)),
        ],
        out_specs=[
            pl.BlockSpec((1, N_CB, IN_TM, LANE), o_map),
            pl.BlockSpec((1, IN_TM, 2 * GATE_RANK), lr_map),
        ],
        out_shape=[
            jax.ShapeDtypeStruct((B, N_CB, L, LANE), _bf16),
            jax.ShapeDtypeStruct((B, L, 2 * GATE_RANK), _f32),
        ],
        scratch_shapes=[pltpu.VMEM((2, IN_TM, D_MODEL), _bf16)],
        compiler_params=pltpu.CompilerParams(
            dimension_semantics=("arbitrary",),
            vmem_limit_bytes=VMEM_LIMIT),
        name="in_proj",
    )(x, norm_w, w_in)


AT_TQ = 1024


def _attn_kernel(sink_ref, q_ref, k_ref, v_ref, z_ref, bias_ref, o_ref, *, nblk):
    h = pl.program_id(1)
    i = pl.program_id(2)
    nsub = AT_TQ // BLOCK
    scale = DH ** -0.5
    col = lax.broadcasted_iota(jnp.int32, (1, 3 * BLOCK), 1)

    def sub(s, c):
        blk = i * nsub + s
        rows = pl.ds(pl.multiple_of(s * BLOCK, BLOCK), BLOCK)

        def kv_rows(j):
            return pl.ds(pl.multiple_of(j * BLOCK, BLOCK), BLOCK)

        lb = jnp.maximum(blk - 1, 0)
        rb = jnp.minimum(blk + 1, nblk - 1)
        kw = jnp.concatenate([k_ref[0, 0, kv_rows(lb), :], k_ref[0, 0, kv_rows(blk), :],
                              k_ref[0, 0, kv_rows(rb), :]], axis=0)
        vw = jnp.concatenate([v_ref[0, 0, kv_rows(lb), :], v_ref[0, 0, kv_rows(blk), :],
                              v_ref[0, 0, kv_rows(rb), :]], axis=0)
        q4 = q_ref[0, :, rows, :].reshape(GROUP * BLOCK, DH)
        s4 = _nt(q4, kw) * scale
        bad = ((col < BLOCK) & (blk == 0)) | ((col >= 2 * BLOCK) & (blk == nblk - 1))
        ps, invs = [], []
        for g in range(GROUP):
            sg = s4[g * BLOCK:(g + 1) * BLOCK] + bias_ref[g]
            sg = jnp.where(bad, NEG, sg)
            sk = sink_ref[h * GROUP + g]
            m = jnp.maximum(jnp.max(sg, axis=-1, keepdims=True), sk)
            p = jnp.exp(sg - m)
            den = jnp.sum(p, axis=-1, keepdims=True) + jnp.exp(sk - m)
            ps.append(p.astype(_bf16))
            invs.append(1.0 / den)
        o4 = _nn(jnp.concatenate(ps, axis=0), vw)
        for g in range(GROUP):
            z = z_ref[0, g, rows, :].astype(_f32)
            o_ref[0, g, rows, :] = (o4[g * BLOCK:(g + 1) * BLOCK] * invs[g] * _silu(z)).astype(_bf16)
        return c

    lax.fori_loop(0, nsub, sub, 0)


def _attention(proj, pos_bias, sink):
    B, _, L, _ = proj.shape
    nblk = L // BLOCK
    tq = min(AT_TQ, L)
    assert tq == AT_TQ
    grid = (B, HKV, L // AT_TQ)
    return pl.pallas_call(
        functools.partial(_attn_kernel, nblk=nblk),
        grid=grid,
        in_specs=[
            pl.BlockSpec(memory_space=pltpu.SMEM),
            pl.BlockSpec((1, GROUP, AT_TQ, LANE), lambda b, h, i: (b, CB_QA // GROUP + h, i, 0)),
            pl.BlockSpec((1, 1, L, LANE), lambda b, h, i: (b, CB_KA + h, 0, 0)),
            pl.BlockSpec((1, 1, L, LANE), lambda b, h, i: (b, CB_VA + h, 0, 0)),
            pl.BlockSpec((1, GROUP, AT_TQ, LANE), lambda b, h, i: (b, CB_ZA // GROUP + h, i, 0)),
            pl.BlockSpec((GROUP, BLOCK, 3 * BLOCK), lambda b, h, i: (h, 0, 0)),
        ],
        out_specs=pl.BlockSpec((1, GROUP, AT_TQ, LANE), lambda b, h, i: (b, h, i, 0)),
        out_shape=jax.ShapeDtypeStruct((B, HA, L, LANE), _bf16),
        compiler_params=pltpu.CompilerParams(
            dimension_semantics=("parallel", "parallel", "parallel"),
            vmem_limit_bytes=VMEM_LIMIT),
        name="attention",
    )(sink, proj, proj, proj, proj, pos_bias)


def _gla_kernel(q_ref, k_ref, v_ref, z_ref, lr_ref, wgf_ref, wgb_ref, bgf_ref, bgb_ref, gn_ref,
                o_ref,
                qd_ref, kd_ref, kt_ref, qs_ref, ks_ref, gdec_ref, vt_ref, st_ref, oacc_ref, *, ngroups):
    R = GROUP_ROWS
    scale = DK ** -0.5
    row = lax.broadcasted_iota(jnp.int32, (R, R), 0)
    colm = lax.broadcasted_iota(jnp.int32, (R, R), 1)
    same_chunk = (row >= CHUNK) == (colm >= CHUNK)
    tri = jnp.where(same_chunk & (colm <= row), 1.0, 0.0).astype(_bf16)
    mask_d = (same_chunk & (colm <= row), same_chunk & (colm >= row))
    mask_o = ((row >= CHUNK) & (colm < CHUNK), (row < CHUNK) & (colm >= CHUNK))
    r1 = lax.broadcasted_iota(jnp.int32, (R, 1), 0)
    second = r1 >= CHUNK

    def grp_rows(g):
        return pl.ds(pl.multiple_of(g * R, R), R)

    def pre(t, c):
        rows = grp_rows(t)
        lr = lr_ref[0, rows, :].astype(_bf16)
        xf = _nn(lr, wgf_ref[0]) + bgf_ref[0]
        xb = _nn(lr, wgb_ref[0]) + bgb_ref[0]
        g2 = jnp.concatenate([_log_sigmoid(xf), _log_sigmoid(xb)], axis=-1) * (1.0 / GATE_NORM)
        hi = g2.astype(_bf16)
        lo = (g2 - hi.astype(_f32)).astype(_bf16)
        pre2 = _nn(tri, hi) + _nn(tri, lo)
        tot = jnp.concatenate(
            [jnp.broadcast_to(pre2[CHUNK - 1:CHUNK], (CHUNK, 2 * DK)),
             jnp.broadcast_to(pre2[R - 1:R], (CHUNK, 2 * DK))], axis=0)
        oth = jnp.concatenate([tot[CHUNK:], tot[:CHUNK]], axis=0)
        gtot = tot + oth
        suf2 = tot - pre2 + g2
        q = q_ref[0, 0, rows, :].astype(_f32) * scale
        k = k_ref[0, 0, rows, :].astype(_f32)
        for d in range(2):
            sl = slice(d * DK, (d + 1) * DK)
            b = (pre2 if d == 0 else suf2)[:, sl]
            extra = jnp.where(second if d == 0 else ~second, oth[:, sl], 0.0)
            bg = b + extra
            qd_ref[d, rows, :] = (q * jnp.exp(b)).astype(_bf16)
            kd_ref[d, rows, :] = (k * jnp.exp(-b)).astype(_bf16)
            kt_ref[d, rows, :] = (k * jnp.exp(tot[:, sl] - b)).astype(_bf16)
            qs_ref[d, rows, :] = (q * jnp.exp(bg)).astype(_bf16)
            ks_ref[d, rows, :] = (k * jnp.exp(gtot[:, sl] - bg)).astype(_bf16)
            gdec_ref[d, t] = jnp.exp(gtot[0:8, sl])
        v2 = jnp.concatenate([v_ref[0, 0, rows, :], v_ref[0, 1, rows, :]], axis=-1).astype(_f32)
        vt_ref[t] = v2.T.astype(_bf16)
        return c

    lax.fori_loop(0, ngroups, pre, 0)

    st_ref[...] = jnp.zeros_like(st_ref)
    gn = gn_ref[0]

    def dir_step(d, g):
        rows = grp_rows(g)
        qd = qd_ref[d, rows, :]
        kk = jnp.concatenate([kd_ref[d, rows, :], kt_ref[d, rows, :]], axis=0)
        p = _nt(qd, kk)
        a = jnp.where(mask_d[d], p[:, :R], jnp.where(mask_o[d], p[:, R:], 0.0)).astype(_bf16)
        v2 = jnp.concatenate([v_ref[0, 0, rows, :], v_ref[0, 1, rows, :]], axis=-1)
        st = st_ref[d]
        o = _nn(a, v2) + _nt(qs_ref[d, rows, :], st.astype(_bf16))
        st_ref[d] = st * gdec_ref[d, g][0:1] + _nn(vt_ref[g], ks_ref[d, rows, :])
        return rows, o

    def first_half(t, c):
        for d in range(2):
            rows, o = dir_step(d, t if d == 0 else ngroups - 1 - t)
            oacc_ref[rows, :] = o
        return c

    def second_half(t, c):
        for d in range(2):
            rows, o = dir_step(d, t if d == 0 else ngroups - 1 - t)
            o = o + oacc_ref[rows, :]
            ms = jnp.mean(o * o, axis=-1, keepdims=True)
            y = o * lax.rsqrt(ms + EPS) * gn
            z = jnp.concatenate([z_ref[0, 0, rows, :], z_ref[0, 1, rows, :]], axis=-1).astype(_f32)
            y = (y * _silu(z)).astype(_bf16)
            o_ref[0, 0, rows, :] = y[:, :LANE]
            o_ref[0, 1, rows, :] = y[:, LANE:]
        return c

    lax.fori_loop(0, ngroups // 2, first_half, 0)
    lax.fori_loop(ngroups // 2, ngroups, second_half, 0)


def _gla(proj, lr, wgf, wgb, bgf, bgb, gla_norm, layer):
    B, _, L, _ = proj.shape
    ngroups = L // GROUP_ROWS
    assert ngroups % 2 == 0
    grid = (B, HB)
    return pl.pallas_call(
        functools.partial(_gla_kernel, ngroups=ngroups),
        grid=grid,
        in_specs=[
            pl.BlockSpec((1, 1, L, LANE), lambda b, h: (b, CB_QB + h, 0, 0)),
            pl.BlockSpec((1, 1, L, LANE), lambda b, h: (b, CB_KB + h, 0, 0)),
            pl.BlockSpec((1, 2, L, LANE), lambda b, h: (b, CB_VB // 2 + h, 0, 0)),
            pl.BlockSpec((1, 2, L, LANE), lambda b, h: (b, CB_ZB // 2 + h, 0, 0)),
            pl.BlockSpec((1, L, 2 * GATE_RANK), lambda b, h: (b, 0, 0)),
            pl.BlockSpec((1, 2 * GATE_RANK, DK), lambda b, h: (layer, 0, h)),
            pl.BlockSpec((1, 2 * GATE_RANK, DK), lambda b, h: (layer, 0, h)),
            pl.BlockSpec((1, 1, DK), lambda b, h: (layer, 0, h)),
            pl.BlockSpec((1, 1, DK), lambda b, h: (layer, 0, h)),
            pl.BlockSpec((1, 1, DV), lambda b, h: (layer, 0, 0)),
        ],
        out_specs=pl.BlockSpec((1, 2, L, LANE), lambda b, h: (b, h, 0, 0)),
        out_shape=jax.ShapeDtypeStruct((B, 2 * HB, L, LANE), _bf16),
        scratch_shapes=[
            pltpu.VMEM((2, L, DK), _bf16),
            pltpu.VMEM((2, L, DK), _bf16),
            pltpu.VMEM((2, L, DK), _bf16),
            pltpu.VMEM((2, L, DK), _bf16),
            pltpu.VMEM((2, L, DK), _bf16),
            pltpu.VMEM((2, ngroups, 8, DK), _f32),
            pltpu.VMEM((ngroups, DV, GROUP_ROWS), _bf16),
            pltpu.VMEM((2, DV, DK), _f32),
            pltpu.VMEM((L, DV), _f32),
        ],
        compiler_params=pltpu.CompilerParams(
            dimension_semantics=("parallel", "parallel"),
            vmem_limit_bytes=VMEM_LIMIT),
        name="gla",
    )(proj, proj, proj, proj, lr, wgf, wgb, bgf, bgb, gla_norm)


OUT_TM = 512


def _out_proj_kernel(a_ref, g_ref, w_ref, x_ref, nw_ref, o_ref):
    mix = jnp.concatenate([a_ref[0, c] for c in range(HA)] + [g_ref[0, c] for c in range(2 * HB)], axis=-1)
    y = _nn(mix, w_ref[0])
    ms = jnp.mean(y * y, axis=-1, keepdims=True)
    o_ref[0] = x_ref[0] + y * lax.rsqrt(ms + EPS) * nw_ref[0]


def _out_proj(attn, gla, w_out, x, norm_w, layer):
    B, L, _ = x.shape
    grid = (B, L // OUT_TM)
    return pl.pallas_call(
        _out_proj_kernel,
        grid=grid,
        in_specs=[
            pl.BlockSpec((1, HA, OUT_TM, LANE), lambda b, i: (b, 0, i, 0)),
            pl.BlockSpec((1, 2 * HB, OUT_TM, LANE), lambda b, i: (b, 0, i, 0)),
            pl.BlockSpec((1, D_MODEL, D_MODEL), lambda b, i: (layer, 0, 0), pipeline_mode=pl.Buffered(1)),
            pl.BlockSpec((1, OUT_TM, D_MODEL), lambda b, i: (b, i, 0)),
            pl.BlockSpec((1, 1, D_MODEL), lambda b, i: (layer, 0, 0)),
        ],
        out_specs=pl.BlockSpec((1, OUT_TM, D_MODEL), lambda b, i: (b, i, 0)),
        out_shape=jax.ShapeDtypeStruct((B, L, D_MODEL), _f32),
        compiler_params=pltpu.CompilerParams(
            dimension_semantics=("parallel", "parallel"),
            vmem_limit_bytes=VMEM_LIMIT),
        name="out_proj",
    )(attn, gla, w_out, x, norm_w)


def _band_tables():
    qi = np.arange(BLOCK)[:, None]
    kj = np.arange(3 * BLOCK)[None, :]
    rel = kj - BLOCK - qi
    nb = N_BUCKETS // 2
    max_exact = nb // 2
    n = np.abs(rel)
    large = max_exact + (np.log(np.maximum(n, 1) / max_exact) / np.log(MAX_DIST / max_exact)
                         * (nb - max_exact)).astype(np.int32)
    large = np.minimum(large, nb - 1)
    bucket = (rel > 0).astype(np.int32) * nb + np.where(n < max_exact, n, large)
    return bucket.astype(np.int32), (n <= WINDOW)


def _position_bias(rel_bias):
    bucket, in_band = _band_tables()
    onehot = jnp.asarray(np.eye(N_BUCKETS, dtype=np.float32)[bucket])
    pos_bias = jnp.einsum("qkn,nh->hqk", onehot, rel_bias.astype(_f32), precision=lax.Precision.HIGHEST)
    return jnp.where(in_band[None], pos_bias, NEG)


def _gate_weights(w_gk_fwd, w_gk_bwd):
    zeros = jnp.zeros_like(w_gk_fwd)
    wgf = jnp.concatenate([w_gk_fwd, zeros], axis=1).astype(_bf16)
    wgb = jnp.concatenate([zeros, w_gk_bwd], axis=1).astype(_bf16)
    return wgf, wgb


def _layer(x, layer, pos_bias, w_in, wgf, wgb, bgf, bgb, sink, gla_norm, w_out, norm_pre, norm_post):
    proj, lr = _in_proj(x, norm_pre, w_in, layer)
    attn = _attention(proj, pos_bias, sink[layer])
    gla = _gla(proj, lr, wgf, wgb, bgf, bgb, gla_norm, layer)
    return _out_proj(attn, gla, w_out, x, norm_post, layer)


def _prepare(rel_bias, w_in, w_gk_fwd, b_gk_fwd, w_gk_bwd, b_gk_bwd, sink, gla_norm, w_out, norm_pre, norm_post):
    wgf, wgb = _gate_weights(w_gk_fwd, w_gk_bwd)
    return (_position_bias(rel_bias), w_in.astype(_bf16), wgf, wgb, b_gk_fwd[:, None, :], b_gk_bwd[:, None, :],
            sink, gla_norm[:, None, :], w_out.astype(_bf16), norm_pre[:, None, :], norm_post[:, None, :])


def kernel(x_prompt, x_sample, rel_bias, w_in, w_gk_fwd, b_gk_fwd, w_gk_bwd, b_gk_bwd, sink, gla_norm, w_out, norm_pre, norm_post):
    params = _prepare(rel_bias, w_in, w_gk_fwd, b_gk_fwd, w_gk_bwd, b_gk_bwd, sink, gla_norm, w_out,
                      norm_pre, norm_post)
    y_prompt, y_sample = x_prompt, x_sample
    for l in range(w_in.shape[0]):
        y_prompt = _layer(y_prompt, l, *params)
        y_sample = _layer(y_sample, l, *params)
    return (y_prompt, y_sample)
```

```python
import functools

import jax
import jax.numpy as jnp
import numpy as np
from jax import lax
from jax.experimental import pallas as pl
from jax.experimental.pallas import tpu as pltpu

D_MODEL = 2048
HA, HKV, DH = 8, 2, 128
GROUP = HA // HKV
WINDOW, BLOCK = 128, 128
N_BUCKETS, MAX_DIST = 32, 128
HB, DK, DV = 4, 128, 256
GATE_RANK, GATE_NORM = 16, 16.0
CHUNK = 64
NEG = -1e30
EPS = 1e-6

LANE = 128
GROUP_ROWS = 2 * CHUNK

WA = HA * DH
WKV = HKV * DH
WBK = HB * DK
WB = HB * DV
N_MAIN = 2 * WA + 2 * WKV + 2 * WBK + 2 * WB
CB_QA = 0
CB_KA = CB_QA + WA // LANE
CB_VA = CB_KA + WKV // LANE
CB_ZA = CB_VA + WKV // LANE
CB_QB = CB_ZA + WA // LANE
CB_KB = CB_QB + WBK // LANE
CB_VB = CB_KB + WBK // LANE
CB_ZB = CB_VB + WB // LANE
N_CB = N_MAIN // LANE

VMEM_LIMIT = 56 * 1024 * 1024

_f32 = jnp.float32
_bf16 = jnp.bfloat16


def _nt(a, b):
    return lax.dot_general(a, b, (((1,), (1,)), ((), ())), preferred_element_type=_f32)


def _nn(a, b):
    return jnp.dot(a, b, preferred_element_type=_f32)


LOG2E = 1.4426950408889634
LN2 = 0.6931471805599453


def _silu(z):
    h = 0.5 * z
    return h + h * jnp.tanh(h)


def _log_sigmoid_scaled(x, s):
    return jnp.minimum(x, 0.0) * s - jnp.log2(1.0 + jnp.exp2(jnp.abs(x) * (-LOG2E))) * (s * LN2)


IN_TM = 512
IN_TN = 512
IN_RC = 128


def _in_proj_kernel(x_ref, nw_ref, w_ref, o_ref, lr_ref, xn_ref):
    g = pl.program_id(0)

    @pl.when(g == 0)
    def _():
        xn_ref[1] = jnp.zeros((IN_TM, D_MODEL), _bf16)

    cur = g % 2
    nw = nw_ref[0]
    for r in range(IN_TM // IN_RC):
        rows = slice(r * IN_RC, (r + 1) * IN_RC)
        x = x_ref[0, rows, :]
        ms = jnp.mean(x * x, axis=-1, keepdims=True)
        xn_ref[cur, rows, :] = (x * lax.rsqrt(ms + EPS) * nw).astype(_bf16)

    xp = xn_ref[1 - cur]
    for j in range(N_MAIN // IN_TN):
        acc = _nn(xp, w_ref[0, :, j * IN_TN:(j + 1) * IN_TN])
        for c in range(IN_TN // LANE):
            o_ref[0, j * (IN_TN // LANE) + c] = acc[:, c * LANE:(c + 1) * LANE].astype(_bf16)
    lr_ref[0] = _nn(xp, w_ref[0, :, N_MAIN:N_MAIN + 2 * GATE_RANK])


def _in_proj(x, norm_w, w_in, layer):
    B, L, _ = x.shape
    nm = L // IN_TM
    steps = B * nm

    def x_map(g):
        t = jnp.minimum(g, steps - 1)
        return (t // nm, t % nm, 0)

    def o_map(g):
        t = jnp.maximum(g - 1, 0)
        return (t // nm, 0, t % nm, 0)

    def lr_map(g):
        t = jnp.maximum(g - 1, 0)
        return (t // nm, t % nm, 0)

    return pl.pallas_call(
        _in_proj_kernel,
        grid=(steps + 1,),
        in_specs=[
            pl.BlockSpec((1, IN_TM, D_MODEL), x_map),
            pl.BlockSpec((1, 1, D_MODEL), lambda g: (layer, 0, 0)),
            pl.BlockSpec((1, D_MODEL, w_in.shape[2]), lambda g: (layer, 0, 0), pipeline_mode=pl.Buffered(1)),
        ],
        out_specs=[
            pl.BlockSpec((1, N_CB, IN_TM, LANE), o_map),
            pl.BlockSpec((1, IN_TM, 2 * GATE_RANK), lr_map),
        ],
        out_shape=[
            jax.ShapeDtypeStruct((B, N_CB, L, LANE), _bf16),
            jax.ShapeDtypeStruct((B, L, 2 * GATE_RANK), _f32),
        ],
        scratch_shapes=[pltpu.VMEM((2, IN_TM, D_MODEL), _bf16)],
        compiler_params=pltpu.CompilerParams(
            dimension_semantics=("arbitrary",),
            vmem_limit_bytes=VMEM_LIMIT),
        name="in_proj",
    )(x, norm_w, w_in)


AT_TQ = 1024


def _attn_kernel(sink_ref, q_ref, k_ref, v_ref, z_ref, bias_ref, o_ref, *, nblk):
    h = pl.program_id(1)
    i = pl.program_id(2)
    nsub = AT_TQ // BLOCK
    scale = DH ** -0.5
    col = lax.broadcasted_iota(jnp.int32, (1, 3 * BLOCK), 1)

    def sub(s, c):
        blk = i * nsub + s
        rows = pl.ds(pl.multiple_of(s * BLOCK, BLOCK), BLOCK)

        def kv_rows(j):
            return pl.ds(pl.multiple_of(j * BLOCK, BLOCK), BLOCK)

        lb = jnp.maximum(blk - 1, 0)
        rb = jnp.minimum(blk + 1, nblk - 1)
        kw = jnp.concatenate([k_ref[0, 0, kv_rows(lb), :], k_ref[0, 0, kv_rows(blk), :],
                              k_ref[0, 0, kv_rows(rb), :]], axis=0)
        vw = jnp.concatenate([v_ref[0, 0, kv_rows(lb), :], v_ref[0, 0, kv_rows(blk), :],
                              v_ref[0, 0, kv_rows(rb), :]], axis=0)
        q4 = q_ref[0, :, rows, :].reshape(GROUP * BLOCK, DH)
        s4 = _nt(q4, kw) * scale
        bad = ((col < BLOCK) & (blk == 0)) | ((col >= 2 * BLOCK) & (blk == nblk - 1))
        ps, invs = [], []
        for g in range(GROUP):
            sg = s4[g * BLOCK:(g + 1) * BLOCK] + bias_ref[g]
            sg = jnp.where(bad, NEG, sg)
            sk = sink_ref[h * GROUP + g]
            m = jnp.maximum(jnp.max(sg, axis=-1, keepdims=True), sk)
            p = jnp.exp(sg - m)
            den = jnp.sum(p, axis=-1, keepdims=True) + jnp.exp(sk - m)
            ps.append(p.astype(_bf16))
            invs.append(1.0 / den)
        o4 = _nn(jnp.concatenate(ps, axis=0), vw)
        for g in range(GROUP):
            z = z_ref[0, g, rows, :].astype(_f32)
            o_ref[0, g, rows, :] = (o4[g * BLOCK:(g + 1) * BLOCK] * invs[g] * _silu(z)).astype(_bf16)
        return c

    lax.fori_loop(0, nsub, sub, 0)


def _attention(proj, pos_bias, sink):
    B, _, L, _ = proj.shape
    nblk = L // BLOCK
    tq = min(AT_TQ, L)
    assert tq == AT_TQ
    grid = (B, HKV, L // AT_TQ)
    return pl.pallas_call(
        functools.partial(_attn_kernel, nblk=nblk),
        grid=grid,
        in_specs=[
            pl.BlockSpec(memory_space=pltpu.SMEM),
            pl.BlockSpec((1, GROUP, AT_TQ, LANE), lambda b, h, i: (b, CB_QA // GROUP + h, i, 0)),
            pl.BlockSpec((1, 1, L, LANE), lambda b, h, i: (b, CB_KA + h, 0, 0)),
            pl.BlockSpec((1, 1, L, LANE), lambda b, h, i: (b, CB_VA + h, 0, 0)),
            pl.BlockSpec((1, GROUP, AT_TQ, LANE), lambda b, h, i: (b, CB_ZA // GROUP + h, i, 0)),
            pl.BlockSpec((GROUP, BLOCK, 3 * BLOCK), lambda b, h, i: (h, 0, 0)),
        ],
        out_specs=pl.BlockSpec((1, GROUP, AT_TQ, LANE), lambda b, h, i: (b, h, i, 0)),
        out_shape=jax.ShapeDtypeStruct((B, HA, L, LANE), _bf16),
        compiler_params=pltpu.CompilerParams(
            dimension_semantics=("parallel", "parallel", "parallel"),
            vmem_limit_bytes=VMEM_LIMIT),
        name="attention",
    )(sink, proj, proj, proj, proj, pos_bias)


GLA_NH = 2


def _tn(a, b):
    return lax.dot_general(a, b, (((0,), (0,)), ((), ())), preferred_element_type=_f32)


def _gla_kernel(q_ref, k_ref, v_ref, z_ref, lr_ref, wgf_ref, wgb_ref, bgf_ref, bgb_ref, gn_ref,
                o_ref,
                qd_ref, kk_ref, qs_ref, ks_ref, gdec_ref, gl_ref, pre_ref, st_ref, oacc_ref, *, ngroups):
    R = GROUP_ROWS
    scale = DK ** -0.5
    row = lax.broadcasted_iota(jnp.int32, (R, R), 0)
    colm = lax.broadcasted_iota(jnp.int32, (R, R), 1)
    same_chunk = (row >= CHUNK) == (colm >= CHUNK)
    tri = jnp.where(same_chunk & (colm <= row), 1.0, 0.0).astype(_bf16)
    mask_d = (same_chunk & (colm <= row), same_chunk & (colm >= row))
    mask_o = ((row >= CHUNK) & (colm < CHUNK), (row < CHUNK) & (colm >= CHUNK))
    wg_refs, bg_refs = (wgf_ref, wgb_ref), (bgf_ref, bgb_ref)

    def grp_rows(g):
        return pl.ds(pl.multiple_of(g * R, R), R)

    def rep(x):
        return jnp.broadcast_to(x, (CHUNK, x.shape[-1]))

    def gates(d, g, slot):
        rows = grp_rows(g)
        lr = lr_ref[0, rows, :].astype(_bf16)
        x = _nn(lr, wg_refs[d][0])
        yield
        gl = _log_sigmoid_scaled(x + bg_refs[d][0], 1.0 / GATE_NORM)
        hi = gl.astype(_bf16)
        lo = (gl - hi.astype(_f32)).astype(_bf16)
        pre = _nn(tri, hi) + _nn(tri, lo)
        yield
        gl_ref[slot, d] = gl
        pre_ref[slot, d] = pre

    def operands(d, g, slot):
        yield
        rows = grp_rows(g)
        gl, pre = gl_ref[slot, d], pre_ref[slot, d]
        t0, t1 = pre[CHUNK - 1:CHUNK], pre[R - 1:R]
        e0, e1 = jnp.exp(t0), jnp.exp(t1)
        tot = jnp.concatenate([rep(t0), rep(t1)], axis=0)
        b = pre if d == 0 else tot - pre + gl
        eb, enb = jnp.exp(b), jnp.exp(-b)
        etot = jnp.concatenate([rep(e0), rep(e1)], axis=0)
        for hh in range(GLA_NH):
            sl = slice(hh * DK, (hh + 1) * DK)
            q = q_ref[0, hh, rows, :].astype(_f32) * scale
            k = k_ref[0, hh, rows, :].astype(_f32)
            qd = q * eb[:, sl]
            kd = k * enb[:, sl]
            kt = kd * etot[:, sl]
            if d == 0:
                qs = jnp.concatenate([qd[:CHUNK], qd[CHUNK:] * e0[:, sl]], axis=0)
                ks = jnp.concatenate([kt[:CHUNK] * e1[:, sl], kt[CHUNK:]], axis=0)
            else:
                qs = jnp.concatenate([qd[:CHUNK] * e1[:, sl], qd[CHUNK:]], axis=0)
                ks = jnp.concatenate([kt[:CHUNK], kt[CHUNK:] * e0[:, sl]], axis=0)
            qd_ref[slot, hh, d] = qd.astype(_bf16)
            kk_ref[slot, hh, d, 0:R, :] = kd.astype(_bf16)
            kk_ref[slot, hh, d, R:2 * R, :] = kt.astype(_bf16)
            qs_ref[slot, hh, d] = qs.astype(_bf16)
            ks_ref[slot, hh, d] = ks.astype(_bf16)
            gdec_ref[slot, hh, d] = jnp.broadcast_to((e0 * e1)[:, sl], (8, DK))

    gn = gn_ref[0]

    def finish(hh, rows, o):
        o = o + oacc_ref[hh, rows, :]
        ms = jnp.mean(o * o, axis=-1, keepdims=True)
        y = o * lax.rsqrt(ms + EPS) * gn
        z = jnp.concatenate([z_ref[0, 2 * hh, rows, :], z_ref[0, 2 * hh + 1, rows, :]], axis=-1).astype(_f32)
        y = (y * _silu(z)).astype(_bf16)
        o_ref[0, 2 * hh, rows, :] = y[:, :LANE]
        o_ref[0, 2 * hh + 1, rows, :] = y[:, LANE:]

    def recurrence(d, hh, trips, second_visit):
        firsts = []
        for t, slot in trips:
            rows = grp_rows(t if d == 0 else ngroups - 1 - t)
            v2 = jnp.concatenate([v_ref[0, 2 * hh, rows, :], v_ref[0, 2 * hh + 1, rows, :]], axis=-1)
            p = _nt(qd_ref[slot, hh, d], kk_ref[slot, hh, d])
            up = _tn(v2, ks_ref[slot, hh, d])
            firsts.append((rows, v2, p, up, qs_ref[slot, hh, d], gdec_ref[slot, hh, d][0:1]))
        yield
        st = st_ref[hh, d]
        outs = []
        for rows, v2, p, up, qs, gdec in firsts:
            a = jnp.where(mask_d[d], p[:, :R], jnp.where(mask_o[d], p[:, R:], 0.0)).astype(_bf16)
            outs.append((rows, _nn(a, v2) + _nt(qs, st.astype(_bf16))))
            st = st * gdec + up
        st_ref[hh, d] = st
        yield
        for rows, o in outs:
            if second_visit:
                finish(hh, rows, o)
            else:
                oacc_ref[hh, rows, :] = o

    def run_staged(tasks):
        tasks = list(tasks)
        while tasks:
            alive = []
            for task in tasks:
                try:
                    next(task)
                    alive.append(task)
                except StopIteration:
                    pass
            tasks = alive

    def per_trip(fn, pair_idx):
        tasks = []
        for slot in range(2):
            t = jnp.minimum(2 * pair_idx + slot, ngroups - 1)
            tasks += [fn(0, t, slot), fn(1, ngroups - 1 - t, slot)]
        return tasks

    st_ref[...] = jnp.zeros_like(st_ref)
    run_staged(per_trip(gates, 0))
    run_staged(per_trip(operands, 0))
    run_staged(per_trip(gates, 1))

    def pair(second_visit):
        def body(tt, c):
            trips = [(2 * tt, 0), (2 * tt + 1, 1)]
            rec = [recurrence(d, hh, trips, second_visit) for d in range(2) for hh in range(GLA_NH)]
            run_staged(rec + per_trip(operands, tt + 1) + per_trip(gates, tt + 2))
            return c
        return body

    lax.fori_loop(0, ngroups // 4, pair(False), 0)
    lax.fori_loop(ngroups // 4, ngroups // 2, pair(True), 0)


def _gla(proj, lr, wgf, wgb, bgf, bgb, gla_norm, layer):
    B, _, L, _ = proj.shape
    ngroups = L // GROUP_ROWS
    assert ngroups % 4 == 0 and HB % GLA_NH == 0
    nh = GLA_NH
    grid = (B, HB // nh)
    slots = (2, nh, 2)
    return pl.pallas_call(
        functools.partial(_gla_kernel, ngroups=ngroups),
        grid=grid,
        in_specs=[
            pl.BlockSpec((1, nh, L, LANE), lambda b, h: (b, CB_QB // nh + h, 0, 0)),
            pl.BlockSpec((1, nh, L, LANE), lambda b, h: (b, CB_KB // nh + h, 0, 0)),
            pl.BlockSpec((1, 2 * nh, L, LANE), lambda b, h: (b, CB_VB // (2 * nh) + h, 0, 0)),
            pl.BlockSpec((1, 2 * nh, L, LANE), lambda b, h: (b, CB_ZB // (2 * nh) + h, 0, 0)),
            pl.BlockSpec((1, L, 2 * GATE_RANK), lambda b, h: (b, 0, 0)),
            pl.BlockSpec((1, 2 * GATE_RANK, nh * DK), lambda b, h: (layer, 0, h)),
            pl.BlockSpec((1, 2 * GATE_RANK, nh * DK), lambda b, h: (layer, 0, h)),
            pl.BlockSpec((1, 1, nh * DK), lambda b, h: (layer, 0, h)),
            pl.BlockSpec((1, 1, nh * DK), lambda b, h: (layer, 0, h)),
            pl.BlockSpec((1, 1, DV), lambda b, h: (layer, 0, 0)),
        ],
        out_specs=pl.BlockSpec((1, 2 * nh, L, LANE), lambda b, h: (b, h, 0, 0)),
        out_shape=jax.ShapeDtypeStruct((B, 2 * HB, L, LANE), _bf16),
        scratch_shapes=[
            pltpu.VMEM(slots + (GROUP_ROWS, DK), _bf16),
            pltpu.VMEM(slots + (2 * GROUP_ROWS, DK), _bf16),
            pltpu.VMEM(slots + (GROUP_ROWS, DK), _bf16),
            pltpu.VMEM(slots + (GROUP_ROWS, DK), _bf16),
            pltpu.VMEM(slots + (8, DK), _f32),
            pltpu.VMEM((2, 2, GROUP_ROWS, nh * DK), _f32),
            pltpu.VMEM((2, 2, GROUP_ROWS, nh * DK), _f32),
            pltpu.VMEM((nh, 2, DV, DK), _f32),
            pltpu.VMEM((nh, L, DV), _f32),
        ],
        compiler_params=pltpu.CompilerParams(
            dimension_semantics=("parallel", "parallel"),
            vmem_limit_bytes=VMEM_LIMIT),
        name="gla",
    )(proj, proj, proj, proj, lr, wgf, wgb, bgf, bgb, gla_norm)


OUT_TM = 512


def _out_proj_kernel(a_ref, g_ref, w_ref, x_ref, nw_ref, o_ref):
    mix = jnp.concatenate([a_ref[0, c] for c in range(HA)] + [g_ref[0, c] for c in range(2 * HB)], axis=-1)
    y = _nn(mix, w_ref[0])
    ms = jnp.mean(y * y, axis=-1, keepdims=True)
    o_ref[0] = x_ref[0] + y * lax.rsqrt(ms + EPS) * nw_ref[0]


def _out_proj(attn, gla, w_out, x, norm_w, layer):
    B, L, _ = x.shape
    grid = (B, L // OUT_TM)
    return pl.pallas_call(
        _out_proj_kernel,
        grid=grid,
        in_specs=[
            pl.BlockSpec((1, HA, OUT_TM, LANE), lambda b, i: (b, 0, i, 0)),
            pl.BlockSpec((1, 2 * HB, OUT_TM, LANE), lambda b, i: (b, 0, i, 0)),
            pl.BlockSpec((1, D_MODEL, D_MODEL), lambda b, i: (layer, 0, 0), pipeline_mode=pl.Buffered(1)),
            pl.BlockSpec((1, OUT_TM, D_MODEL), lambda b, i: (b, i, 0)),
            pl.BlockSpec((1, 1, D_MODEL), lambda b, i: (layer, 0, 0)),
        ],
        out_specs=pl.BlockSpec((1, OUT_TM, D_MODEL), lambda b, i: (b, i, 0)),
        out_shape=jax.ShapeDtypeStruct((B, L, D_MODEL), _f32),
        compiler_params=pltpu.CompilerParams(
            dimension_semantics=("parallel", "parallel"),
            vmem_limit_bytes=VMEM_LIMIT),
        name="out_proj",
    )(attn, gla, w_out, x, norm_w)


def _band_tables():
    qi = np.arange(BLOCK)[:, None]
    kj = np.arange(3 * BLOCK)[None, :]
    rel = kj - BLOCK - qi
    nb = N_BUCKETS // 2
    max_exact = nb // 2
    n = np.abs(rel)
    large = max_exact + (np.log(np.maximum(n, 1) / max_exact) / np.log(MAX_DIST / max_exact)
                         * (nb - max_exact)).astype(np.int32)
    large = np.minimum(large, nb - 1)
    bucket = (rel > 0).astype(np.int32) * nb + np.where(n < max_exact, n, large)
    return bucket.astype(np.int32), (n <= WINDOW)


def _position_bias(rel_bias):
    bucket, in_band = _band_tables()
    onehot = jnp.asarray(np.eye(N_BUCKETS, dtype=np.float32)[bucket])
    pos_bias = jnp.einsum("qkn,nh->hqk", onehot, rel_bias.astype(_f32), precision=lax.Precision.HIGHEST)
    return jnp.where(in_band[None], pos_bias, NEG)


def _gate_weights(w_gk_fwd, w_gk_bwd):
    zeros = jnp.zeros_like(w_gk_fwd)
    wgf = jnp.concatenate([w_gk_fwd, zeros], axis=1).astype(_bf16)
    wgb = jnp.concatenate([zeros, w_gk_bwd], axis=1).astype(_bf16)
    return wgf, wgb


def _layer(x, layer, pos_bias, w_in, wgf, wgb, bgf, bgb, sink, gla_norm, w_out, norm_pre, norm_post):
    proj, lr = _in_proj(x, norm_pre, w_in, layer)
    attn = _attention(proj, pos_bias, sink[layer])
    gla = _gla(proj, lr, wgf, wgb, bgf, bgb, gla_norm, layer)
    return _out_proj(attn, gla, w_out, x, norm_post, layer)


def _prepare(rel_bias, w_in, w_gk_fwd, b_gk_fwd, w_gk_bwd, b_gk_bwd, sink, gla_norm, w_out, norm_pre, norm_post):
    wgf, wgb = _gate_weights(w_gk_fwd, w_gk_bwd)
    return (_position_bias(rel_bias), w_in.astype(_bf16), wgf, wgb, b_gk_fwd[:, None, :], b_gk_bwd[:, None, :],
            sink, gla_norm[:, None, :], w_out.astype(_bf16), norm_pre[:, None, :], norm_post[:, None, :])


def kernel(x_prompt, x_sample, rel_bias, w_in, w_gk_fwd, b_gk_fwd, w_gk_bwd, b_gk_bwd, sink, gla_norm, w_out, norm_pre, norm_post):
    params = _prepare(rel_bias, w_in, w_gk_fwd, b_gk_fwd, w_gk_bwd, b_gk_bwd, sink, gla_norm, w_out,
                      norm_pre, norm_post)
    y_prompt, y_sample = x_prompt, x_sample
    for l in range(w_in.shape[0]):
        y_prompt = _layer(y_prompt, l, *params)
        y_sample = _layer(y_sample, l, *params)
    return (y_prompt, y_sample)
```

```python
import functools

import jax
import jax.numpy as jnp
import numpy as np
from jax import lax
from jax.experimental import pallas as pl
from jax.experimental.pallas import tpu as pltpu

D_MODEL = 2048
HA, HKV, DH = 8, 2, 128
GROUP = HA // HKV
WINDOW, BLOCK = 128, 128
N_BUCKETS, MAX_DIST = 32, 128
HB, DK, DV = 4, 128, 256
GATE_RANK, GATE_NORM = 16, 16.0
CHUNK = 64
NEG = -1e30
EPS = 1e-6

LANE = 128
GROUP_ROWS = 2 * CHUNK

WA = HA * DH
WKV = HKV * DH
WBK = HB * DK
WB = HB * DV
N_MAIN = 2 * WA + 2 * WKV + 2 * WBK + 2 * WB
CB_QA = 0
CB_KA = CB_QA + WA // LANE
CB_VA = CB_KA + WKV // LANE
CB_ZA = CB_VA + WKV // LANE
CB_QB = CB_ZA + WA // LANE
CB_KB = CB_QB + WBK // LANE
CB_VB = CB_KB + WBK // LANE
CB_ZB = CB_VB + WB // LANE
N_CB = N_MAIN // LANE

VMEM_LIMIT = 56 * 1024 * 1024

_f32 = jnp.float32
_bf16 = jnp.bfloat16


def _nt(a, b):
    return lax.dot_general(a, b, (((1,), (1,)), ((), ())), preferred_element_type=_f32)


def _nn(a, b):
    return jnp.dot(a, b, preferred_element_type=_f32)


LOG2E = 1.4426950408889634
LN2 = 0.6931471805599453


def _silu(z):
    h = 0.5 * z
    return h + h * jnp.tanh(h)


def _log_sigmoid_scaled(x, s):
    return jnp.minimum(x, 0.0) * s - jnp.log2(1.0 + jnp.exp2(jnp.abs(x) * (-LOG2E))) * (s * LN2)


IN_TH = 256
IN_TN = 512
IN_RC = 128


def _in_proj_kernel(x0_ref, xa_ref, xb_ref, nw_ref, w_ref, o_ref, lr_ref, xn_ref):
    g = pl.program_id(0)
    nw = nw_ref[0]

    def normalise(x_ref, slot):
        for r in range(IN_TH // IN_RC):
            rows = slice(r * IN_RC, (r + 1) * IN_RC)
            x = x_ref[0, rows, :]
            ms = jnp.mean(x * x, axis=-1, keepdims=True)
            xn_ref[slot, rows, :] = (x * lax.rsqrt(ms + EPS) * nw).astype(_bf16)

    def project(slot):
        rows = slice(slot * IN_TH, (slot + 1) * IN_TH)
        xp = xn_ref[slot]
        for j in range(N_MAIN // IN_TN):
            acc = _nn(xp, w_ref[0, :, j * IN_TN:(j + 1) * IN_TN])
            for c in range(IN_TN // LANE):
                o_ref[0, j * (IN_TN // LANE) + c, rows, :] = acc[:, c * LANE:(c + 1) * LANE].astype(_bf16)
        lr_ref[0, rows, :] = _nn(xp, w_ref[0, :, N_MAIN:N_MAIN + 2 * GATE_RANK])

    @pl.when(g == 0)
    def _():
        normalise(x0_ref, 0)

    normalise(xa_ref, 1)
    project(0)
    normalise(xb_ref, 0)
    project(1)


def _in_proj(x, norm_w, w_in, layer):
    B, L, _ = x.shape
    nh = L // IN_TH
    halves = B * nh
    steps = halves // 2

    def half_map(offset):
        def index(g):
            t = jnp.minimum(2 * g + offset, halves - 1)
            return (t // nh, t % nh, 0)
        return index

    half_spec = (1, IN_TH, D_MODEL)
    return pl.pallas_call(
        _in_proj_kernel,
        grid=(steps,),
        in_specs=[
            pl.BlockSpec(half_spec, lambda g: (0, 0, 0)),
            pl.BlockSpec(half_spec, half_map(1)),
            pl.BlockSpec(half_spec, half_map(2)),
            pl.BlockSpec((1, 1, D_MODEL), lambda g: (layer, 0, 0)),
            pl.BlockSpec((1, D_MODEL, w_in.shape[2]), lambda g: (layer, 0, 0), pipeline_mode=pl.Buffered(1)),
        ],
        out_specs=[
            pl.BlockSpec((1, N_CB, 2 * IN_TH, LANE), lambda g: (g // (nh // 2), 0, g % (nh // 2), 0)),
            pl.BlockSpec((1, 2 * IN_TH, 2 * GATE_RANK), lambda g: (g // (nh // 2), g % (nh // 2), 0)),
        ],
        out_shape=[
            jax.ShapeDtypeStruct((B, N_CB, L, LANE), _bf16),
            jax.ShapeDtypeStruct((B, L, 2 * GATE_RANK), _f32),
        ],
        scratch_shapes=[pltpu.VMEM((2, IN_TH, D_MODEL), _bf16)],
        compiler_params=pltpu.CompilerParams(
            dimension_semantics=("arbitrary",),
            vmem_limit_bytes=VMEM_LIMIT),
        name="in_proj",
    )(x, x, x, norm_w, w_in)


def _attn_kernel(sink_ref, q_ref, k_ref, v_ref, z_ref, bias_ref, o_ref, s_ref, p_ref, t_ref, *, nblk):
    h = pl.program_id(1)
    s_ref[...] = jnp.zeros_like(s_ref)
    p_ref[...] = jnp.zeros_like(p_ref)
    t_ref[...] = jnp.zeros_like(t_ref)

    def blk_rows(j):
        return pl.ds(pl.multiple_of(j * BLOCK, BLOCK), BLOCK)

    def window(ref, blk):
        lb = jnp.maximum(blk - 1, 0)
        rb = jnp.minimum(blk + 1, nblk - 1)
        return jnp.concatenate([ref[0, 0, blk_rows(lb), :], ref[0, 0, blk_rows(blk), :],
                                ref[0, 0, blk_rows(rb), :]], axis=0)

    def stage_a(blk, slot):
        q4 = q_ref[0, :, blk_rows(blk), :].reshape(GROUP * BLOCK, DH)
        s_ref[slot] = _nt(q4, window(k_ref, blk))

    def stage_b(blk, src, dst):
        table = jnp.where(blk == 0, 1, jnp.where(blk == nblk - 1, 2, 0))
        for g in range(GROUP):
            gr = slice(g * BLOCK, (g + 1) * BLOCK)
            sg = s_ref[src, gr, :] + bias_ref[table, g]
            sk = sink_ref[h * GROUP + g] * LOG2E
            m = jnp.maximum(jnp.max(sg, axis=-1, keepdims=True), sk)
            p_ref[dst, gr, :] = jnp.exp2(sg - m).astype(_bf16)
            t_ref[dst, g] = jnp.broadcast_to(jnp.exp2(sk - m), (BLOCK, DH))

    def stage_c(blk, src):
        vw = window(v_ref, blk)
        vw = jnp.concatenate([vw, jnp.ones_like(vw)], axis=-1)
        o4 = _nn(p_ref[src], vw)
        rows = blk_rows(blk)
        for g in range(GROUP):
            og = o4[g * BLOCK:(g + 1) * BLOCK]
            z = z_ref[0, g, rows, :].astype(_f32)
            den = og[:, DH:] + t_ref[src, g]
            o_ref[0, g, rows, :] = (og[:, :DH] * (1.0 / den) * _silu(z)).astype(_bf16)

    def iteration(it, par):
        stage_c(jnp.clip(it - 2, 0, nblk - 1), 1 - par)
        stage_b(jnp.clip(it - 1, 0, nblk - 1), 1 - par, par)
        stage_a(jnp.minimum(it, nblk - 1), par)

    def body(j, c):
        iteration(2 * j, 0)
        iteration(2 * j + 1, 1)
        return c

    lax.fori_loop(0, (nblk + 2) // 2, body, 0)


def _attention(proj, pos_bias, sink):
    B, _, L, _ = proj.shape
    nblk = L // BLOCK
    assert nblk >= 2 and nblk % 2 == 0
    grid = (B, HKV)
    return pl.pallas_call(
        functools.partial(_attn_kernel, nblk=nblk),
        grid=grid,
        in_specs=[
            pl.BlockSpec(memory_space=pltpu.SMEM),
            pl.BlockSpec((1, GROUP, L, LANE), lambda b, h: (b, CB_QA // GROUP + h, 0, 0)),
            pl.BlockSpec((1, 1, L, LANE), lambda b, h: (b, CB_KA + h, 0, 0)),
            pl.BlockSpec((1, 1, L, LANE), lambda b, h: (b, CB_VA + h, 0, 0)),
            pl.BlockSpec((1, GROUP, L, LANE), lambda b, h: (b, CB_ZA // GROUP + h, 0, 0)),
            pl.BlockSpec((3, GROUP, BLOCK, 3 * BLOCK), lambda b, h: (0, h, 0, 0)),
        ],
        out_specs=pl.BlockSpec((1, GROUP, L, LANE), lambda b, h: (b, h, 0, 0)),
        out_shape=jax.ShapeDtypeStruct((B, HA, L, LANE), _bf16),
        scratch_shapes=[
            pltpu.VMEM((2, GROUP * BLOCK, 3 * BLOCK), _f32),
            pltpu.VMEM((2, GROUP * BLOCK, 3 * BLOCK), _bf16),
            pltpu.VMEM((2, GROUP, BLOCK, DH), _f32),
        ],
        compiler_params=pltpu.CompilerParams(
            dimension_semantics=("parallel", "parallel"),
            vmem_limit_bytes=VMEM_LIMIT),
        name="attention",
    )(sink, proj, proj, proj, proj, pos_bias)


GLA_NH = 2


def _tn(a, b):
    return lax.dot_general(a, b, (((0,), (0,)), ((), ())), preferred_element_type=_f32)


def _gla_kernel(q_ref, k_ref, v_ref, z_ref, lr_ref, wgf_ref, wgb_ref, bgf_ref, bgb_ref, gn_ref,
                o_ref,
                qd_ref, kk_ref, qs_ref, ks_ref, gdec_ref, gl_ref, pre_ref, st_ref, oacc_ref, *, ngroups):
    R = GROUP_ROWS
    row = lax.broadcasted_iota(jnp.int32, (R, R), 0)
    colm = lax.broadcasted_iota(jnp.int32, (R, R), 1)
    same_chunk = (row >= CHUNK) == (colm >= CHUNK)
    tri = jnp.where(same_chunk & (colm <= row), 1.0, 0.0).astype(_bf16)
    mask_d = (same_chunk & (colm <= row), same_chunk & (colm >= row))
    mask_o = ((row >= CHUNK) & (colm < CHUNK), (row < CHUNK) & (colm >= CHUNK))
    wg_refs, bg_refs = (wgf_ref, wgb_ref), (bgf_ref, bgb_ref)

    def grp_rows(g):
        return pl.ds(pl.multiple_of(g * R, R), R)

    def rep(x):
        return jnp.broadcast_to(x, (CHUNK, x.shape[-1]))

    def gates(d, g, slot):
        rows = grp_rows(g)
        lr = lr_ref[0, rows, :].astype(_bf16)
        x = _nn(lr, wg_refs[d][0])
        yield
        gl = _log_sigmoid_scaled(x + bg_refs[d][0], 1.0 / GATE_NORM)
        hi = gl.astype(_bf16)
        lo = (gl - hi.astype(_f32)).astype(_bf16)
        pre = _nn(tri, hi) + _nn(tri, lo)
        yield
        gl_ref[slot, d] = gl
        pre_ref[slot, d] = pre

    def operands(d, g, slot):
        yield
        rows = grp_rows(g)
        gl, pre = gl_ref[slot, d], pre_ref[slot, d]
        t0, t1 = pre[CHUNK - 1:CHUNK], pre[R - 1:R]
        e0, e1 = jnp.exp(t0), jnp.exp(t1)
        tot = jnp.concatenate([rep(t0), rep(t1)], axis=0)
        b = pre if d == 0 else tot - pre + gl
        eb, enb = jnp.exp(b), jnp.exp(-b)
        etot = jnp.concatenate([rep(e0), rep(e1)], axis=0)
        for hh in range(GLA_NH):
            sl = slice(hh * DK, (hh + 1) * DK)
            q = q_ref[0, hh, rows, :].astype(_f32)
            k = k_ref[0, hh, rows, :].astype(_f32)
            qd = q * eb[:, sl]
            kd = k * enb[:, sl]
            kt = kd * etot[:, sl]
            if d == 0:
                qs = jnp.concatenate([qd[:CHUNK], qd[CHUNK:] * e0[:, sl]], axis=0)
                ks = jnp.concatenate([kt[:CHUNK] * e1[:, sl], kt[CHUNK:]], axis=0)
            else:
                qs = jnp.concatenate([qd[:CHUNK] * e1[:, sl], qd[CHUNK:]], axis=0)
                ks = jnp.concatenate([kt[:CHUNK], kt[CHUNK:] * e0[:, sl]], axis=0)
            qd_ref[slot, hh, d] = qd.astype(_bf16)
            kk_ref[slot, hh, d, 0:R, :] = kd.astype(_bf16)
            kk_ref[slot, hh, d, R:2 * R, :] = kt.astype(_bf16)
            qs_ref[slot, hh, d] = qs.astype(_bf16)
            ks_ref[slot, hh, d] = ks.astype(_bf16)
            gdec_ref[slot, hh, d] = jnp.broadcast_to((e0 * e1)[:, sl], (8, DK))

    gn = gn_ref[0]

    def finish(hh, rows, o):
        o = o + oacc_ref[hh, rows, :]
        ms = jnp.mean(o * o, axis=-1, keepdims=True)
        y = o * lax.rsqrt(ms + EPS) * gn
        z = jnp.concatenate([z_ref[0, 2 * hh, rows, :], z_ref[0, 2 * hh + 1, rows, :]], axis=-1).astype(_f32)
        y = (y * _silu(z)).astype(_bf16)
        o_ref[0, 2 * hh, rows, :] = y[:, :LANE]
        o_ref[0, 2 * hh + 1, rows, :] = y[:, LANE:]

    def recurrence(d, hh, trips, second_visit):
        firsts = []
        for t, slot in trips:
            rows = grp_rows(t if d == 0 else ngroups - 1 - t)
            v2 = jnp.concatenate([v_ref[0, 2 * hh, rows, :], v_ref[0, 2 * hh + 1, rows, :]], axis=-1)
            p = _nt(qd_ref[slot, hh, d], kk_ref[slot, hh, d])
            up = _tn(v2, ks_ref[slot, hh, d])
            firsts.append((rows, v2, p, up, qs_ref[slot, hh, d], gdec_ref[slot, hh, d][0:1]))
        yield
        st = st_ref[hh, d]
        outs = []
        for rows, v2, p, up, qs, gdec in firsts:
            a = jnp.where(mask_d[d], p[:, :R], jnp.where(mask_o[d], p[:, R:], 0.0)).astype(_bf16)
            outs.append((rows, _nn(a, v2) + _nt(qs, st.astype(_bf16))))
            st = st * gdec + up
        st_ref[hh, d] = st
        yield
        for rows, o in outs:
            if second_visit:
                finish(hh, rows, o)
            else:
                oacc_ref[hh, rows, :] = o

    def run_staged(tasks):
        tasks = list(tasks)
        while tasks:
            alive = []
            for task in tasks:
                try:
                    next(task)
                    alive.append(task)
                except StopIteration:
                    pass
            tasks = alive

    def per_trip(fn, pair_idx):
        tasks = []
        for slot in range(2):
            t = jnp.minimum(2 * pair_idx + slot, ngroups - 1)
            tasks += [fn(0, t, slot), fn(1, ngroups - 1 - t, slot)]
        return tasks

    st_ref[...] = jnp.zeros_like(st_ref)
    run_staged(per_trip(gates, 0))
    run_staged(per_trip(operands, 0))
    run_staged(per_trip(gates, 1))

    def pair(second_visit):
        def body(tt, c):
            trips = [(2 * tt, 0), (2 * tt + 1, 1)]
            rec = [recurrence(d, hh, trips, second_visit) for d in range(2) for hh in range(GLA_NH)]
            run_staged(rec + per_trip(operands, tt + 1) + per_trip(gates, tt + 2))
            return c
        return body

    lax.fori_loop(0, ngroups // 4, pair(False), 0)
    lax.fori_loop(ngroups // 4, ngroups // 2, pair(True), 0)


def _gla(proj, lr, wgf, wgb, bgf, bgb, gla_norm, layer):
    B, _, L, _ = proj.shape
    ngroups = L // GROUP_ROWS
    assert ngroups % 4 == 0 and HB % GLA_NH == 0
    nh = GLA_NH
    grid = (B, HB // nh)
    slots = (2, nh, 2)
    return pl.pallas_call(
        functools.partial(_gla_kernel, ngroups=ngroups),
        grid=grid,
        in_specs=[
            pl.BlockSpec((1, nh, L, LANE), lambda b, h: (b, CB_QB // nh + h, 0, 0)),
            pl.BlockSpec((1, nh, L, LANE), lambda b, h: (b, CB_KB // nh + h, 0, 0)),
            pl.BlockSpec((1, 2 * nh, L, LANE), lambda b, h: (b, CB_VB // (2 * nh) + h, 0, 0)),
            pl.BlockSpec((1, 2 * nh, L, LANE), lambda b, h: (b, CB_ZB // (2 * nh) + h, 0, 0)),
            pl.BlockSpec((1, L, 2 * GATE_RANK), lambda b, h: (b, 0, 0)),
            pl.BlockSpec((1, 2 * GATE_RANK, nh * DK), lambda b, h: (layer, 0, h)),
            pl.BlockSpec((1, 2 * GATE_RANK, nh * DK), lambda b, h: (layer, 0, h)),
            pl.BlockSpec((1, 1, nh * DK), lambda b, h: (layer, 0, h)),
            pl.BlockSpec((1, 1, nh * DK), lambda b, h: (layer, 0, h)),
            pl.BlockSpec((1, 1, DV), lambda b, h: (layer, 0, 0)),
        ],
        out_specs=pl.BlockSpec((1, 2 * nh, L, LANE), lambda b, h: (b, h, 0, 0)),
        out_shape=jax.ShapeDtypeStruct((B, 2 * HB, L, LANE), _bf16),
        scratch_shapes=[
            pltpu.VMEM(slots + (GROUP_ROWS, DK), _bf16),
            pltpu.VMEM(slots + (2 * GROUP_ROWS, DK), _bf16),
            pltpu.VMEM(slots + (GROUP_ROWS, DK), _bf16),
            pltpu.VMEM(slots + (GROUP_ROWS, DK), _bf16),
            pltpu.VMEM(slots + (8, DK), _f32),
            pltpu.VMEM((2, 2, GROUP_ROWS, nh * DK), _f32),
            pltpu.VMEM((2, 2, GROUP_ROWS, nh * DK), _f32),
            pltpu.VMEM((nh, 2, DV, DK), _f32),
            pltpu.VMEM((nh, L, DV), _f32),
        ],
        compiler_params=pltpu.CompilerParams(
            dimension_semantics=("parallel", "parallel"),
            vmem_limit_bytes=VMEM_LIMIT),
        name="gla",
    )(proj, proj, proj, proj, lr, wgf, wgb, bgf, bgb, gla_norm)


OUT_TM = 512


def _out_proj_kernel(a_ref, g_ref, w_ref, x_ref, nw_ref, o_ref):
    mix = jnp.concatenate([a_ref[0, c] for c in range(HA)] + [g_ref[0, c] for c in range(2 * HB)], axis=-1)
    y = _nn(mix, w_ref[0])
    ms = jnp.mean(y * y, axis=-1, keepdims=True)
    o_ref[0] = x_ref[0] + y * lax.rsqrt(ms + EPS) * nw_ref[0]


def _out_proj(attn, gla, w_out, x, norm_w, layer):
    B, L, _ = x.shape
    grid = (B, L // OUT_TM)
    return pl.pallas_call(
        _out_proj_kernel,
        grid=grid,
        in_specs=[
            pl.BlockSpec((1, HA, OUT_TM, LANE), lambda b, i: (b, 0, i, 0)),
            pl.BlockSpec((1, 2 * HB, OUT_TM, LANE), lambda b, i: (b, 0, i, 0)),
            pl.BlockSpec((1, D_MODEL, D_MODEL), lambda b, i: (layer, 0, 0), pipeline_mode=pl.Buffered(1)),
            pl.BlockSpec((1, OUT_TM, D_MODEL), lambda b, i: (b, i, 0)),
            pl.BlockSpec((1, 1, D_MODEL), lambda b, i: (layer, 0, 0)),
        ],
        out_specs=pl.BlockSpec((1, OUT_TM, D_MODEL), lambda b, i: (b, i, 0)),
        out_shape=jax.ShapeDtypeStruct((B, L, D_MODEL), _f32),
        compiler_params=pltpu.CompilerParams(
            dimension_semantics=("parallel", "parallel"),
            vmem_limit_bytes=VMEM_LIMIT),
        name="out_proj",
    )(attn, gla, w_out, x, norm_w)


def _band_tables():
    qi = np.arange(BLOCK)[:, None]
    kj = np.arange(3 * BLOCK)[None, :]
    rel = kj - BLOCK - qi
    nb = N_BUCKETS // 2
    max_exact = nb // 2
    n = np.abs(rel)
    large = max_exact + (np.log(np.maximum(n, 1) / max_exact) / np.log(MAX_DIST / max_exact)
                         * (nb - max_exact)).astype(np.int32)
    large = np.minimum(large, nb - 1)
    bucket = (rel > 0).astype(np.int32) * nb + np.where(n < max_exact, n, large)
    return bucket.astype(np.int32), (n <= WINDOW)


def _position_bias(rel_bias):
    bucket, in_band = _band_tables()
    onehot = jnp.asarray(np.eye(N_BUCKETS, dtype=np.float32)[bucket])
    pos_bias = jnp.einsum("qkn,nh->hqk", onehot, rel_bias.astype(_f32), precision=lax.Precision.HIGHEST)
    key_block = np.arange(3 * BLOCK)[None, :] // BLOCK
    tables = [jnp.where((in_band & keep)[None], pos_bias * LOG2E, NEG)
              for keep in (key_block >= 0, key_block >= 1, key_block <= 1)]
    return jnp.stack(tables, axis=0)


def _gate_weights(w_gk_fwd, w_gk_bwd):
    zeros = jnp.zeros_like(w_gk_fwd)
    wgf = jnp.concatenate([w_gk_fwd, zeros], axis=1).astype(_bf16)
    wgb = jnp.concatenate([zeros, w_gk_bwd], axis=1).astype(_bf16)
    return wgf, wgb


def _layer(x, layer, pos_bias, w_in, wgf, wgb, bgf, bgb, sink, gla_norm, w_out, norm_pre, norm_post):
    proj, lr = _in_proj(x, norm_pre, w_in, layer)
    attn = _attention(proj, pos_bias, sink[layer])
    gla = _gla(proj, lr, wgf, wgb, bgf, bgb, gla_norm, layer)
    return _out_proj(attn, gla, w_out, x, norm_post, layer)


def _prepare(rel_bias, w_in, w_gk_fwd, b_gk_fwd, w_gk_bwd, b_gk_bwd, sink, gla_norm, w_out, norm_pre, norm_post):
    wgf, wgb = _gate_weights(w_gk_fwd, w_gk_bwd)
    col_scale = np.ones((w_in.shape[2],), np.float32)
    col_scale[CB_QA * LANE:CB_QA * LANE + WA] = DH ** -0.5 * LOG2E
    col_scale[CB_QB * LANE:CB_QB * LANE + WBK] = DK ** -0.5
    w_in = (w_in * col_scale).astype(_bf16)
    return (_position_bias(rel_bias), w_in, wgf, wgb, b_gk_fwd[:, None, :], b_gk_bwd[:, None, :],
            sink, gla_norm[:, None, :], w_out.astype(_bf16), norm_pre[:, None, :], norm_post[:, None, :])


def kernel(x_prompt, x_sample, rel_bias, w_in, w_gk_fwd, b_gk_fwd, w_gk_bwd, b_gk_bwd, sink, gla_norm, w_out, norm_pre, norm_post):
    params = _prepare(rel_bias, w_in, w_gk_fwd, b_gk_fwd, w_gk_bwd, b_gk_bwd, sink, gla_norm, w_out,
                      norm_pre, norm_post)
    y_prompt, y_sample = x_prompt, x_sample
    for l in range(w_in.shape[0]):
        y_prompt = _layer(y_prompt, l, *params)
        y_sample = _layer(y_sample, l, *params)
    return (y_prompt, y_sample)
```

```python
import functools

import jax
import jax.numpy as jnp
import numpy as np
from jax import lax
from jax.experimental import pallas as pl
from jax.experimental.pallas import tpu as pltpu

D_MODEL = 2048
HA, HKV, DH = 8, 2, 128
GROUP = HA // HKV
WINDOW, BLOCK = 128, 128
N_BUCKETS, MAX_DIST = 32, 128
HB, DK, DV = 4, 128, 256
GATE_RANK, GATE_NORM = 16, 16.0
CHUNK = 64
NEG = -1e30
EPS = 1e-6

LANE = 128
GROUP_ROWS = 2 * CHUNK

WA = HA * DH
WKV = HKV * DH
WBK = HB * DK
WB = HB * DV
N_MAIN = 2 * WA + 2 * WKV + 2 * WBK + 2 * WB
CB_QA = 0
CB_KA = CB_QA + WA // LANE
CB_VA = CB_KA + WKV // LANE
CB_ZA = CB_VA + WKV // LANE
CB_QB = CB_ZA + WA // LANE
CB_KB = CB_QB + WBK // LANE
CB_VB = CB_KB + WBK // LANE
CB_ZB = CB_VB + WB // LANE
N_CB = N_MAIN // LANE

VMEM_LIMIT = 56 * 1024 * 1024

_f32 = jnp.float32
_bf16 = jnp.bfloat16


def _nt(a, b):
    return lax.dot_general(a, b, (((1,), (1,)), ((), ())), preferred_element_type=_f32)


def _nn(a, b):
    return jnp.dot(a, b, preferred_element_type=_f32)


LOG2E = 1.4426950408889634
LN2 = 0.6931471805599453


def _silu(z):
    h = 0.5 * z
    return h + h * jnp.tanh(h)


def _log_sigmoid_scaled(x, s):
    return jnp.minimum(x, 0.0) * s - jnp.log2(1.0 + jnp.exp2(jnp.abs(x) * (-LOG2E))) * (s * LN2)


IN_TH = 256
IN_TN = 512
IN_RC = 128


def _in_proj_kernel(x0_ref, xa_ref, xb_ref, nw_ref, w_ref, o_ref, lr_ref, xn_ref):
    g = pl.program_id(0)
    nw = nw_ref[0]

    def normalise(x_ref, slot):
        for r in range(IN_TH // IN_RC):
            rows = slice(r * IN_RC, (r + 1) * IN_RC)
            x = x_ref[0, rows, :]
            ms = jnp.mean(x * x, axis=-1, keepdims=True)
            xn_ref[slot, rows, :] = (x * lax.rsqrt(ms + EPS) * nw).astype(_bf16)

    def project(slot):
        rows = slice(slot * IN_TH, (slot + 1) * IN_TH)
        xp = xn_ref[slot]
        for j in range(N_MAIN // IN_TN):
            acc = _nt(xp, w_ref[0, j * IN_TN:(j + 1) * IN_TN, :])
            for c in range(IN_TN // LANE):
                o_ref[0, j * (IN_TN // LANE) + c, rows, :] = acc[:, c * LANE:(c + 1) * LANE].astype(_bf16)
        lr_ref[0, rows, :] = _nt(xp, w_ref[0, N_MAIN:N_MAIN + 2 * GATE_RANK, :])

    @pl.when(g == 0)
    def _():
        normalise(x0_ref, 0)

    normalise(xa_ref, 1)
    project(0)
    normalise(xb_ref, 0)
    project(1)


def _in_proj(x, norm_w, w_in, layer):
    B, L, _ = x.shape
    nh = L // IN_TH
    halves = B * nh
    steps = halves // 2

    def half_map(offset):
        def index(g):
            t = jnp.minimum(2 * g + offset, halves - 1)
            return (t // nh, t % nh, 0)
        return index

    half_spec = (1, IN_TH, D_MODEL)
    return pl.pallas_call(
        _in_proj_kernel,
        grid=(steps,),
        in_specs=[
            pl.BlockSpec(half_spec, lambda g: (0, 0, 0)),
            pl.BlockSpec(half_spec, half_map(1)),
            pl.BlockSpec(half_spec, half_map(2)),
            pl.BlockSpec((1, 1, D_MODEL), lambda g: (layer, 0, 0)),
            pl.BlockSpec((1, w_in.shape[1], D_MODEL), lambda g: (layer, 0, 0), pipeline_mode=pl.Buffered(1)),
        ],
        out_specs=[
            pl.BlockSpec((1, N_CB, 2 * IN_TH, LANE), lambda g: (g // (nh // 2), 0, g % (nh // 2), 0)),
            pl.BlockSpec((1, 2 * IN_TH, 2 * GATE_RANK), lambda g: (g // (nh // 2), g % (nh // 2), 0)),
        ],
        out_shape=[
            jax.ShapeDtypeStruct((B, N_CB, L, LANE), _bf16),
            jax.ShapeDtypeStruct((B, L, 2 * GATE_RANK), _f32),
        ],
        scratch_shapes=[pltpu.VMEM((2, IN_TH, D_MODEL), _bf16)],
        compiler_params=pltpu.CompilerParams(
            dimension_semantics=("arbitrary",),
            vmem_limit_bytes=VMEM_LIMIT),
        name="in_proj",
    )(x, x, x, norm_w, w_in)


def _attn_kernel(sink_ref, q_ref, k_ref, v_ref, z_ref, bias_ref, o_ref, s_ref, p_ref, t_ref, *, nblk):
    h = pl.program_id(1)
    s_ref[...] = jnp.zeros_like(s_ref)
    p_ref[...] = jnp.zeros_like(p_ref)
    t_ref[...] = jnp.zeros_like(t_ref)

    def blk_rows(j):
        return pl.ds(pl.multiple_of(j * BLOCK, BLOCK), BLOCK)

    def window(ref, blk):
        lb = jnp.maximum(blk - 1, 0)
        rb = jnp.minimum(blk + 1, nblk - 1)
        return jnp.concatenate([ref[0, 0, blk_rows(lb), :], ref[0, 0, blk_rows(blk), :],
                                ref[0, 0, blk_rows(rb), :]], axis=0)

    def stage_a(blk, slot):
        q4 = q_ref[0, :, blk_rows(blk), :].reshape(GROUP * BLOCK, DH)
        s_ref[slot] = _nt(q4, window(k_ref, blk))

    def stage_b(blk, src, dst):
        table = jnp.where(blk == 0, 1, jnp.where(blk == nblk - 1, 2, 0))
        for g in range(GROUP):
            gr = slice(g * BLOCK, (g + 1) * BLOCK)
            sg = s_ref[src, gr, :] + bias_ref[table, g]
            sk = sink_ref[h * GROUP + g] * LOG2E
            m = jnp.maximum(jnp.max(sg, axis=-1, keepdims=True), sk)
            p_ref[dst, gr, :] = jnp.exp2(sg - m).astype(_bf16)
            t_ref[dst, g] = jnp.broadcast_to(jnp.exp2(sk - m), (BLOCK, DH))

    def stage_c(blk, src):
        vw = window(v_ref, blk)
        vw = jnp.concatenate([vw, jnp.ones_like(vw)], axis=-1)
        o4 = _nn(p_ref[src], vw)
        rows = blk_rows(blk)
        for g in range(GROUP):
            og = o4[g * BLOCK:(g + 1) * BLOCK]
            z = z_ref[0, g, rows, :].astype(_f32)
            den = og[:, DH:] + t_ref[src, g]
            o_ref[0, g, rows, :] = (og[:, :DH] * (1.0 / den) * _silu(z)).astype(_bf16)

    def iteration(it, par):
        stage_c(jnp.clip(it - 2, 0, nblk - 1), 1 - par)
        stage_b(jnp.clip(it - 1, 0, nblk - 1), 1 - par, par)
        stage_a(jnp.minimum(it, nblk - 1), par)

    def body(j, c):
        iteration(2 * j, 0)
        iteration(2 * j + 1, 1)
        return c

    lax.fori_loop(0, (nblk + 2) // 2, body, 0)


def _attention(proj, pos_bias, sink):
    B, _, L, _ = proj.shape
    nblk = L // BLOCK
    assert nblk >= 2 and nblk % 2 == 0
    grid = (B, HKV)
    return pl.pallas_call(
        functools.partial(_attn_kernel, nblk=nblk),
        grid=grid,
        in_specs=[
            pl.BlockSpec(memory_space=pltpu.SMEM),
            pl.BlockSpec((1, GROUP, L, LANE), lambda b, h: (b, CB_QA // GROUP + h, 0, 0)),
            pl.BlockSpec((1, 1, L, LANE), lambda b, h: (b, CB_KA + h, 0, 0)),
            pl.BlockSpec((1, 1, L, LANE), lambda b, h: (b, CB_VA + h, 0, 0)),
            pl.BlockSpec((1, GROUP, L, LANE), lambda b, h: (b, CB_ZA // GROUP + h, 0, 0)),
            pl.BlockSpec((3, GROUP, BLOCK, 3 * BLOCK), lambda b, h: (0, h, 0, 0)),
        ],
        out_specs=pl.BlockSpec((1, GROUP, L, LANE), lambda b, h: (b, h, 0, 0)),
        out_shape=jax.ShapeDtypeStruct((B, HA, L, LANE), _bf16),
        scratch_shapes=[
            pltpu.VMEM((2, GROUP * BLOCK, 3 * BLOCK), _f32),
            pltpu.VMEM((2, GROUP * BLOCK, 3 * BLOCK), _bf16),
            pltpu.VMEM((2, GROUP, BLOCK, DH), _f32),
        ],
        compiler_params=pltpu.CompilerParams(
            dimension_semantics=("parallel", "parallel"),
            vmem_limit_bytes=VMEM_LIMIT),
        name="attention",
    )(sink, proj, proj, proj, proj, pos_bias)


GLA_NH = 2


def _tn(a, b):
    return lax.dot_general(a, b, (((0,), (0,)), ((), ())), preferred_element_type=_f32)


def _gla_kernel(q_ref, k_ref, v_ref, z_ref, lr_ref, wgf_ref, wgb_ref, bgf_ref, bgb_ref, gn_ref,
                o_ref,
                qd_ref, kk_ref, qs_ref, ks_ref, gdec_ref, vt_ref, gl_ref, pre_ref, st_ref, oacc_ref, *, ngroups):
    R = GROUP_ROWS
    row = lax.broadcasted_iota(jnp.int32, (R, R), 0)
    colm = lax.broadcasted_iota(jnp.int32, (R, R), 1)
    same_chunk = (row >= CHUNK) == (colm >= CHUNK)
    tri = jnp.where(same_chunk & (colm <= row), 1.0, 0.0).astype(_bf16)
    mask_d = (same_chunk & (colm <= row), same_chunk & (colm >= row))
    mask_o = ((row >= CHUNK) & (colm < CHUNK), (row < CHUNK) & (colm >= CHUNK))
    wg_refs, bg_refs = (wgf_ref, wgb_ref), (bgf_ref, bgb_ref)

    def grp_rows(g):
        return pl.ds(pl.multiple_of(g * R, R), R)

    def rep(x):
        return jnp.broadcast_to(x, (CHUNK, x.shape[-1]))

    def gates(d, g, slot):
        rows = grp_rows(g)
        lr = lr_ref[0, rows, :].astype(_bf16)
        x = _nn(lr, wg_refs[d][0])
        yield
        gl = _log_sigmoid_scaled(x + bg_refs[d][0], 1.0 / GATE_NORM)
        hi = gl.astype(_bf16)
        lo = (gl - hi.astype(_f32)).astype(_bf16)
        pre = _nn(tri, hi) + _nn(tri, lo)
        yield
        gl_ref[slot, d] = gl
        pre_ref[slot, d] = pre

    def operands(d, g, slot):
        yield
        rows = grp_rows(g)
        gl, pre = gl_ref[slot, d], pre_ref[slot, d]
        t0, t1 = pre[CHUNK - 1:CHUNK], pre[R - 1:R]
        e0, e1 = jnp.exp(t0), jnp.exp(t1)
        tot = jnp.concatenate([rep(t0), rep(t1)], axis=0)
        b = pre if d == 0 else tot - pre + gl
        eb, enb = jnp.exp(b), jnp.exp(-b)
        etot = jnp.concatenate([rep(e0), rep(e1)], axis=0)
        for hh in range(GLA_NH):
            sl = slice(hh * DK, (hh + 1) * DK)
            q = q_ref[0, hh, rows, :].astype(_f32)
            k = k_ref[0, hh, rows, :].astype(_f32)
            qd = q * eb[:, sl]
            kd = k * enb[:, sl]
            kt = kd * etot[:, sl]
            if d == 0:
                qs = jnp.concatenate([qd[:CHUNK], qd[CHUNK:] * e0[:, sl]], axis=0)
                ks = jnp.concatenate([kt[:CHUNK] * e1[:, sl], kt[CHUNK:]], axis=0)
            else:
                qs = jnp.concatenate([qd[:CHUNK] * e1[:, sl], qd[CHUNK:]], axis=0)
                ks = jnp.concatenate([kt[:CHUNK], kt[CHUNK:] * e0[:, sl]], axis=0)
            qd_ref[slot, hh, d] = qd.astype(_bf16)
            kk_ref[slot, hh, d, 0:R, :] = kd.astype(_bf16)
            kk_ref[slot, hh, d, R:2 * R, :] = kt.astype(_bf16)
            qs_ref[slot, hh, d] = qs.astype(_bf16)
            ks_ref[slot, hh, d] = ks.astype(_bf16)
            gdec_ref[slot, hh, d] = jnp.broadcast_to((e0 * e1)[:, sl], (8, DK))
            v2 = jnp.concatenate([v_ref[0, 2 * hh, rows, :], v_ref[0, 2 * hh + 1, rows, :]], axis=-1)
            vt_ref[slot, hh, d] = v2.T

    gn = gn_ref[0]

    def finish(hh, rows, o):
        o = o + oacc_ref[hh, rows, :]
        ms = jnp.mean(o * o, axis=-1, keepdims=True)
        y = o * lax.rsqrt(ms + EPS) * gn
        z = jnp.concatenate([z_ref[0, 2 * hh, rows, :], z_ref[0, 2 * hh + 1, rows, :]], axis=-1).astype(_f32)
        y = (y * _silu(z)).astype(_bf16)
        o_ref[0, 2 * hh, rows, :] = y[:, :LANE]
        o_ref[0, 2 * hh + 1, rows, :] = y[:, LANE:]

    def recurrence(d, hh, trips, second_visit):
        firsts = []
        for t, slot in trips:
            rows = grp_rows(t if d == 0 else ngroups - 1 - t)
            v2 = jnp.concatenate([v_ref[0, 2 * hh, rows, :], v_ref[0, 2 * hh + 1, rows, :]], axis=-1)
            p = _nt(qd_ref[slot, hh, d], kk_ref[slot, hh, d])
            up = _nn(vt_ref[slot, hh, d], ks_ref[slot, hh, d])
            firsts.append((rows, v2, p, up, qs_ref[slot, hh, d], gdec_ref[slot, hh, d][0:1]))
        yield
        st = st_ref[hh, d]
        outs = []
        for rows, v2, p, up, qs, gdec in firsts:
            a = jnp.where(mask_d[d], p[:, :R], jnp.where(mask_o[d], p[:, R:], 0.0)).astype(_bf16)
            outs.append((rows, _nn(a, v2) + _nt(qs, st.astype(_bf16))))
            st = st * gdec + up
        st_ref[hh, d] = st
        yield
        for rows, o in outs:
            if second_visit:
                finish(hh, rows, o)
            else:
                oacc_ref[hh, rows, :] = o

    def run_staged(tasks):
        tasks = list(tasks)
        while tasks:
            alive = []
            for task in tasks:
                try:
                    next(task)
                    alive.append(task)
                except StopIteration:
                    pass
            tasks = alive

    def per_trip(fn, pair_idx):
        tasks = []
        for slot in range(2):
            t = jnp.minimum(2 * pair_idx + slot, ngroups - 1)
            tasks += [fn(0, t, slot), fn(1, ngroups - 1 - t, slot)]
        return tasks

    st_ref[...] = jnp.zeros_like(st_ref)
    run_staged(per_trip(gates, 0))
    run_staged(per_trip(operands, 0))
    run_staged(per_trip(gates, 1))

    def pair(second_visit):
        def body(tt, c):
            trips = [(2 * tt, 0), (2 * tt + 1, 1)]
            rec = [recurrence(d, hh, trips, second_visit) for d in range(2) for hh in range(GLA_NH)]
            run_staged(per_trip(gates, tt + 2) + rec + per_trip(operands, tt + 1))
            return c
        return body

    lax.fori_loop(0, ngroups // 4, pair(False), 0)
    lax.fori_loop(ngroups // 4, ngroups // 2, pair(True), 0)


def _gla(proj, lr, wgf, wgb, bgf, bgb, gla_norm, layer):
    B, _, L, _ = proj.shape
    ngroups = L // GROUP_ROWS
    assert ngroups % 4 == 0 and HB % GLA_NH == 0
    nh = GLA_NH
    grid = (B, HB // nh)
    slots = (2, nh, 2)
    return pl.pallas_call(
        functools.partial(_gla_kernel, ngroups=ngroups),
        grid=grid,
        in_specs=[
            pl.BlockSpec((1, nh, L, LANE), lambda b, h: (b, CB_QB // nh + h, 0, 0)),
            pl.BlockSpec((1, nh, L, LANE), lambda b, h: (b, CB_KB // nh + h, 0, 0)),
            pl.BlockSpec((1, 2 * nh, L, LANE), lambda b, h: (b, CB_VB // (2 * nh) + h, 0, 0)),
            pl.BlockSpec((1, 2 * nh, L, LANE), lambda b, h: (b, CB_ZB // (2 * nh) + h, 0, 0)),
            pl.BlockSpec((1, L, 2 * GATE_RANK), lambda b, h: (b, 0, 0)),
            pl.BlockSpec((1, 2 * GATE_RANK, nh * DK), lambda b, h: (layer, 0, h)),
            pl.BlockSpec((1, 2 * GATE_RANK, nh * DK), lambda b, h: (layer, 0, h)),
            pl.BlockSpec((1, 1, nh * DK), lambda b, h: (layer, 0, h)),
            pl.BlockSpec((1, 1, nh * DK), lambda b, h: (layer, 0, h)),
            pl.BlockSpec((1, 1, DV), lambda b, h: (layer, 0, 0)),
        ],
        out_specs=pl.BlockSpec((1, 2 * nh, L, LANE), lambda b, h: (b, h, 0, 0)),
        out_shape=jax.ShapeDtypeStruct((B, 2 * HB, L, LANE), _bf16),
        scratch_shapes=[
            pltpu.VMEM(slots + (GROUP_ROWS, DK), _bf16),
            pltpu.VMEM(slots + (2 * GROUP_ROWS, DK), _bf16),
            pltpu.VMEM(slots + (GROUP_ROWS, DK), _bf16),
            pltpu.VMEM(slots + (GROUP_ROWS, DK), _bf16),
            pltpu.VMEM(slots + (8, DK), _f32),
            pltpu.VMEM(slots + (DV, GROUP_ROWS), _bf16),
            pltpu.VMEM((2, 2, GROUP_ROWS, nh * DK), _f32),
            pltpu.VMEM((2, 2, GROUP_ROWS, nh * DK), _f32),
            pltpu.VMEM((nh, 2, DV, DK), _f32),
            pltpu.VMEM((nh, L, DV), _f32),
        ],
        compiler_params=pltpu.CompilerParams(
            dimension_semantics=("parallel", "parallel"),
            vmem_limit_bytes=VMEM_LIMIT),
        name="gla",
    )(proj, proj, proj, proj, lr, wgf, wgb, bgf, bgb, gla_norm)


OUT_TM = 512


def _out_proj_kernel(a_ref, g_ref, w_ref, x_ref, nw_ref, o_ref):
    mix = jnp.concatenate([a_ref[0, c] for c in range(HA)] + [g_ref[0, c] for c in range(2 * HB)], axis=-1)
    y = _nn(mix, w_ref[0])
    ms = jnp.mean(y * y, axis=-1, keepdims=True)
    o_ref[0] = x_ref[0] + y * lax.rsqrt(ms + EPS) * nw_ref[0]


def _out_proj(attn, gla, w_out, x, norm_w, layer):
    B, L, _ = x.shape
    grid = (B, L // OUT_TM)
    return pl.pallas_call(
        _out_proj_kernel,
        grid=grid,
        in_specs=[
            pl.BlockSpec((1, HA, OUT_TM, LANE), lambda b, i: (b, 0, i, 0)),
            pl.BlockSpec((1, 2 * HB, OUT_TM, LANE), lambda b, i: (b, 0, i, 0)),
            pl.BlockSpec((1, D_MODEL, D_MODEL), lambda b, i: (layer, 0, 0), pipeline_mode=pl.Buffered(1)),
            pl.BlockSpec((1, OUT_TM, D_MODEL), lambda b, i: (b, i, 0)),
            pl.BlockSpec((1, 1, D_MODEL), lambda b, i: (layer, 0, 0)),
        ],
        out_specs=pl.BlockSpec((1, OUT_TM, D_MODEL), lambda b, i: (b, i, 0)),
        out_shape=jax.ShapeDtypeStruct((B, L, D_MODEL), _f32),
        compiler_params=pltpu.CompilerParams(
            dimension_semantics=("parallel", "parallel"),
            vmem_limit_bytes=VMEM_LIMIT),
        name="out_proj",
    )(attn, gla, w_out, x, norm_w)


def _band_tables():
    qi = np.arange(BLOCK)[:, None]
    kj = np.arange(3 * BLOCK)[None, :]
    rel = kj - BLOCK - qi
    nb = N_BUCKETS // 2
    max_exact = nb // 2
    n = np.abs(rel)
    large = max_exact + (np.log(np.maximum(n, 1) / max_exact) / np.log(MAX_DIST / max_exact)
                         * (nb - max_exact)).astype(np.int32)
    large = np.minimum(large, nb - 1)
    bucket = (rel > 0).astype(np.int32) * nb + np.where(n < max_exact, n, large)
    return bucket.astype(np.int32), (n <= WINDOW)


def _position_bias(rel_bias):
    bucket, in_band = _band_tables()
    onehot = jnp.asarray(np.eye(N_BUCKETS, dtype=np.float32)[bucket])
    pos_bias = jnp.einsum("qkn,nh->hqk", onehot, rel_bias.astype(_f32), precision=lax.Precision.HIGHEST)
    key_block = np.arange(3 * BLOCK)[None, :] // BLOCK
    tables = [jnp.where((in_band & keep)[None], pos_bias * LOG2E, NEG)
              for keep in (key_block >= 0, key_block >= 1, key_block <= 1)]
    return jnp.stack(tables, axis=0)


def _gate_weights(w_gk_fwd, w_gk_bwd):
    zeros = jnp.zeros_like(w_gk_fwd)
    wgf = jnp.concatenate([w_gk_fwd, zeros], axis=1).astype(_bf16)
    wgb = jnp.concatenate([zeros, w_gk_bwd], axis=1).astype(_bf16)
    return wgf, wgb


def _layer(x, layer, pos_bias, w_in, wgf, wgb, bgf, bgb, sink, gla_norm, w_out, norm_pre, norm_post):
    proj, lr = _in_proj(x, norm_pre, w_in, layer)
    attn = _attention(proj, pos_bias, sink[layer])
    gla = _gla(proj, lr, wgf, wgb, bgf, bgb, gla_norm, layer)
    return _out_proj(attn, gla, w_out, x, norm_post, layer)


def _prepare(rel_bias, w_in, w_gk_fwd, b_gk_fwd, w_gk_bwd, b_gk_bwd, sink, gla_norm, w_out, norm_pre, norm_post):
    wgf, wgb = _gate_weights(w_gk_fwd, w_gk_bwd)
    col_scale = np.ones((w_in.shape[2],), np.float32)
    col_scale[CB_QA * LANE:CB_QA * LANE + WA] = DH ** -0.5 * LOG2E
    col_scale[CB_QB * LANE:CB_QB * LANE + WBK] = DK ** -0.5
    w_in = jnp.swapaxes(w_in * col_scale, 1, 2).astype(_bf16)
    return (_position_bias(rel_bias), w_in, wgf, wgb, b_gk_fwd[:, None, :], b_gk_bwd[:, None, :],
            sink, gla_norm[:, None, :], w_out.astype(_bf16), norm_pre[:, None, :], norm_post[:, None, :])


def kernel(x_prompt, x_sample, rel_bias, w_in, w_gk_fwd, b_gk_fwd, w_gk_bwd, b_gk_bwd, sink, gla_norm, w_out, norm_pre, norm_post):
    params = _prepare(rel_bias, w_in, w_gk_fwd, b_gk_fwd, w_gk_bwd, b_gk_bwd, sink, gla_norm, w_out,
                      norm_pre, norm_post)
    y_prompt, y_sample = x_prompt, x_sample
    for l in range(w_in.shape[0]):
        y_prompt = _layer(y_prompt, l, *params)
        y_sample = _layer(y_sample, l, *params)
    return (y_prompt, y_sample)
```

```python
import functools

import jax
import jax.numpy as jnp
import numpy as np
from jax import lax
from jax.experimental import pallas as pl
from jax.experimental.pallas import tpu as pltpu

D_MODEL = 2048
HA, HKV, DH = 8, 2, 128
GROUP = HA // HKV
WINDOW, BLOCK = 128, 128
N_BUCKETS, MAX_DIST = 32, 128
HB, DK, DV = 4, 128, 256
GATE_RANK, GATE_NORM = 16, 16.0
CHUNK = 64
NEG = -1e30
EPS = 1e-6

LANE = 128
GROUP_ROWS = 2 * CHUNK

WA = HA * DH
WKV = HKV * DH
WBK = HB * DK
WB = HB * DV
N_MAIN = 2 * WA + 2 * WKV + 2 * WBK + 2 * WB
CB_QA = 0
CB_KA = CB_QA + WA // LANE
CB_VA = CB_KA + WKV // LANE
CB_ZA = CB_VA + WKV // LANE
CB_QB = CB_ZA + WA // LANE
CB_KB = CB_QB + WBK // LANE
CB_VB = CB_KB + WBK // LANE
CB_ZB = CB_VB + WB // LANE
N_CB = N_MAIN // LANE

VMEM_LIMIT = 56 * 1024 * 1024

_f32 = jnp.float32
_bf16 = jnp.bfloat16


def _nt(a, b):
    return lax.dot_general(a, b, (((1,), (1,)), ((), ())), preferred_element_type=_f32)


def _nn(a, b):
    return jnp.dot(a, b, preferred_element_type=_f32)


LOG2E = 1.4426950408889634
LN2 = 0.6931471805599453


def _silu(z):
    h = 0.5 * z
    return h + h * jnp.tanh(h)


def _log_sigmoid_scaled(x, s):
    return jnp.minimum(x, 0.0) * s - jnp.log2(1.0 + jnp.exp2(jnp.abs(x) * (-LOG2E))) * (s * LN2)


IN_TH = 256
IN_TN = 512
IN_RC = 128


def _in_proj_kernel(x0_ref, xa_ref, xb_ref, nw_ref, w_ref, o_ref, lr_ref, xn_ref):
    g = pl.program_id(0)
    nw = nw_ref[0]

    def normalise(x_ref, slot):
        for r in range(IN_TH // IN_RC):
            rows = slice(r * IN_RC, (r + 1) * IN_RC)
            x = x_ref[0, rows, :]
            ms = jnp.mean(x * x, axis=-1, keepdims=True)
            xn_ref[slot, rows, :] = (x * lax.rsqrt(ms + EPS) * nw).astype(_bf16)

    def project(slot):
        rows = slice(slot * IN_TH, (slot + 1) * IN_TH)
        xp = xn_ref[slot]
        for j in range(N_MAIN // IN_TN):
            acc = _nt(xp, w_ref[0, j * IN_TN:(j + 1) * IN_TN, :])
            for c in range(IN_TN // LANE):
                o_ref[0, j * (IN_TN // LANE) + c, rows, :] = acc[:, c * LANE:(c + 1) * LANE].astype(_bf16)
        lr_ref[0, rows, :] = _nt(xp, w_ref[0, N_MAIN:N_MAIN + 2 * GATE_RANK, :])

    @pl.when(g == 0)
    def _():
        normalise(x0_ref, 0)

    normalise(xa_ref, 1)
    project(0)
    normalise(xb_ref, 0)
    project(1)


def _in_proj(x, norm_w, w_in, layer):
    B, L, _ = x.shape
    nh = L // IN_TH
    halves = B * nh
    steps = halves // 2

    def half_map(offset):
        def index(g):
            t = jnp.minimum(2 * g + offset, halves - 1)
            return (t // nh, t % nh, 0)
        return index

    half_spec = (1, IN_TH, D_MODEL)
    return pl.pallas_call(
        _in_proj_kernel,
        grid=(steps,),
        in_specs=[
            pl.BlockSpec(half_spec, lambda g: (0, 0, 0)),
            pl.BlockSpec(half_spec, half_map(1)),
            pl.BlockSpec(half_spec, half_map(2)),
            pl.BlockSpec((1, 1, D_MODEL), lambda g: (layer, 0, 0)),
            pl.BlockSpec((1, w_in.shape[1], D_MODEL), lambda g: (layer, 0, 0), pipeline_mode=pl.Buffered(1)),
        ],
        out_specs=[
            pl.BlockSpec((1, N_CB, 2 * IN_TH, LANE), lambda g: (g // (nh // 2), 0, g % (nh // 2), 0)),
            pl.BlockSpec((1, 2 * IN_TH, 2 * GATE_RANK), lambda g: (g // (nh // 2), g % (nh // 2), 0)),
        ],
        out_shape=[
            jax.ShapeDtypeStruct((B, N_CB, L, LANE), _bf16),
            jax.ShapeDtypeStruct((B, L, 2 * GATE_RANK), _f32),
        ],
        scratch_shapes=[pltpu.VMEM((2, IN_TH, D_MODEL), _bf16)],
        compiler_params=pltpu.CompilerParams(
            dimension_semantics=("arbitrary",),
            vmem_limit_bytes=VMEM_LIMIT),
        name="in_proj",
    )(x, x, x, norm_w, w_in)


def _attn_kernel(sink_ref, q_ref, k_ref, v_ref, z_ref, bias_ref, o_ref, s_ref, p_ref, t_ref, *, nblk):
    h = pl.program_id(1)
    def blk_rows(j):
        return slice(j * BLOCK, (j + 1) * BLOCK)

    def window(ref, blk):
        lb = max(blk - 1, 0)
        rb = min(blk + 1, nblk - 1)
        return jnp.concatenate([ref[0, 0, blk_rows(lb), :], ref[0, 0, blk_rows(blk), :],
                                ref[0, 0, blk_rows(rb), :]], axis=0)

    def stage_a(blk, slot):
        q4 = q_ref[0, :, blk_rows(blk), :].reshape(GROUP * BLOCK, DH)
        s_ref[slot] = _nt(q4, window(k_ref, blk))

    def stage_b(blk, src, dst):
        table = 1 if blk == 0 else (2 if blk == nblk - 1 else 0)
        for g in range(GROUP):
            gr = slice(g * BLOCK, (g + 1) * BLOCK)
            sg = s_ref[src, gr, :] + bias_ref[table, g]
            sk = sink_ref[h * GROUP + g] * LOG2E
            m = jnp.maximum(jnp.max(sg, axis=-1, keepdims=True), sk)
            p_ref[dst, gr, :] = jnp.exp2(sg - m).astype(_bf16)
            t_ref[dst, g] = jnp.broadcast_to(jnp.exp2(sk - m), (BLOCK, DH))

    def stage_c(blk, src):
        vw = window(v_ref, blk)
        vw = jnp.concatenate([vw, jnp.ones_like(vw)], axis=-1)
        o4 = _nn(p_ref[src], vw)
        rows = blk_rows(blk)
        for g in range(GROUP):
            og = o4[g * BLOCK:(g + 1) * BLOCK]
            z = z_ref[0, g, rows, :].astype(_f32)
            den = og[:, DH:] + t_ref[src, g]
            o_ref[0, g, rows, :] = (og[:, :DH] * (1.0 / den) * _silu(z)).astype(_bf16)

    for it in range(nblk + 2):
        par = it % 2
        if 0 <= it - 2 < nblk:
            stage_c(it - 2, 1 - par)
        if 0 <= it - 1 < nblk:
            stage_b(it - 1, 1 - par, par)
        if it < nblk:
            stage_a(it, par)


def _attention(proj, pos_bias, sink):
    B, _, L, _ = proj.shape
    nblk = L // BLOCK
    assert nblk >= 2 and nblk % 2 == 0
    grid = (B, HKV)
    return pl.pallas_call(
        functools.partial(_attn_kernel, nblk=nblk),
        grid=grid,
        in_specs=[
            pl.BlockSpec(memory_space=pltpu.SMEM),
            pl.BlockSpec((1, GROUP, L, LANE), lambda b, h: (b, CB_QA // GROUP + h, 0, 0)),
            pl.BlockSpec((1, 1, L, LANE), lambda b, h: (b, CB_KA + h, 0, 0)),
            pl.BlockSpec((1, 1, L, LANE), lambda b, h: (b, CB_VA + h, 0, 0)),
            pl.BlockSpec((1, GROUP, L, LANE), lambda b, h: (b, CB_ZA // GROUP + h, 0, 0)),
            pl.BlockSpec((3, GROUP, BLOCK, 3 * BLOCK), lambda b, h: (0, h, 0, 0)),
        ],
        out_specs=pl.BlockSpec((1, GROUP, L, LANE), lambda b, h: (b, h, 0, 0)),
        out_shape=jax.ShapeDtypeStruct((B, HA, L, LANE), _bf16),
        scratch_shapes=[
            pltpu.VMEM((2, GROUP * BLOCK, 3 * BLOCK), _f32),
            pltpu.VMEM((2, GROUP * BLOCK, 3 * BLOCK), _bf16),
            pltpu.VMEM((2, GROUP, BLOCK, DH), _f32),
        ],
        compiler_params=pltpu.CompilerParams(
            dimension_semantics=("parallel", "parallel"),
            vmem_limit_bytes=VMEM_LIMIT),
        name="attention",
    )(sink, proj, proj, proj, proj, pos_bias)


GLA_NH = 2


def _tn(a, b):
    return lax.dot_general(a, b, (((0,), (0,)), ((), ())), preferred_element_type=_f32)


def _gla_kernel(q_ref, k_ref, v_ref, z_ref, lr_ref, wgf_ref, wgb_ref, bgf_ref, bgb_ref, gn_ref,
                o_ref,
                qd_ref, kk_ref, qs_ref, ks_ref, gdec_ref, vt_ref, gl_ref, pre_ref, st_ref, oacc_ref, *, ngroups):
    R = GROUP_ROWS
    row = lax.broadcasted_iota(jnp.int32, (R, R), 0)
    colm = lax.broadcasted_iota(jnp.int32, (R, R), 1)
    same_chunk = (row >= CHUNK) == (colm >= CHUNK)
    tri = jnp.where(same_chunk & (colm <= row), 1.0, 0.0).astype(_bf16)
    mask_d = (same_chunk & (colm <= row), same_chunk & (colm >= row))
    mask_o = ((row >= CHUNK) & (colm < CHUNK), (row < CHUNK) & (colm >= CHUNK))
    wg_refs, bg_refs = (wgf_ref, wgb_ref), (bgf_ref, bgb_ref)

    def grp_rows(g):
        return slice(g * R, (g + 1) * R)

    def rep(x):
        return jnp.broadcast_to(x, (CHUNK, x.shape[-1]))

    def gates(d, g, slot):
        rows = grp_rows(g)
        lr = lr_ref[0, rows, :].astype(_bf16)
        x = _nn(lr, wg_refs[d][0])
        yield
        gl = _log_sigmoid_scaled(x + bg_refs[d][0], 1.0 / GATE_NORM)
        hi = gl.astype(_bf16)
        lo = (gl - hi.astype(_f32)).astype(_bf16)
        pre = _nn(tri, hi) + _nn(tri, lo)
        yield
        gl_ref[slot, d] = gl
        pre_ref[slot, d] = pre

    def operands(d, g, slot):
        yield
        rows = grp_rows(g)
        gl, pre = gl_ref[slot, d], pre_ref[slot, d]
        t0, t1 = pre[CHUNK - 1:CHUNK], pre[R - 1:R]
        e0, e1 = jnp.exp(t0), jnp.exp(t1)
        tot = jnp.concatenate([rep(t0), rep(t1)], axis=0)
        b = pre if d == 0 else tot - pre + gl
        eb, enb = jnp.exp(b), jnp.exp(-b)
        etot = jnp.concatenate([rep(e0), rep(e1)], axis=0)
        for hh in range(GLA_NH):
            sl = slice(hh * DK, (hh + 1) * DK)
            q = q_ref[0, hh, rows, :].astype(_f32)
            k = k_ref[0, hh, rows, :].astype(_f32)
            qd = q * eb[:, sl]
            kd = k * enb[:, sl]
            kt = kd * etot[:, sl]
            if d == 0:
                qs = jnp.concatenate([qd[:CHUNK], qd[CHUNK:] * e0[:, sl]], axis=0)
                ks = jnp.concatenate([kt[:CHUNK] * e1[:, sl], kt[CHUNK:]], axis=0)
            else:
                qs = jnp.concatenate([qd[:CHUNK] * e1[:, sl], qd[CHUNK:]], axis=0)
                ks = jnp.concatenate([kt[:CHUNK], kt[CHUNK:] * e0[:, sl]], axis=0)
            qd_ref[slot, hh, d] = qd.astype(_bf16)
            kk_ref[slot, hh, d, 0:R, :] = kd.astype(_bf16)
            kk_ref[slot, hh, d, R:2 * R, :] = kt.astype(_bf16)
            qs_ref[slot, hh, d] = qs.astype(_bf16)
            ks_ref[slot, hh, d] = ks.astype(_bf16)
            gdec_ref[slot, hh, d] = jnp.broadcast_to((e0 * e1)[:, sl], (8, DK))
            v2 = jnp.concatenate([v_ref[0, 2 * hh, rows, :], v_ref[0, 2 * hh + 1, rows, :]], axis=-1)
            vt_ref[slot, hh, d] = v2.T

    gn = gn_ref[0]

    def finish(hh, rows, o):
        o = o + oacc_ref[hh, rows, :]
        ms = jnp.mean(o * o, axis=-1, keepdims=True)
        y = o * lax.rsqrt(ms + EPS) * gn
        z = jnp.concatenate([z_ref[0, 2 * hh, rows, :], z_ref[0, 2 * hh + 1, rows, :]], axis=-1).astype(_f32)
        y = (y * _silu(z)).astype(_bf16)
        o_ref[0, 2 * hh, rows, :] = y[:, :LANE]
        o_ref[0, 2 * hh + 1, rows, :] = y[:, LANE:]

    def recurrence(d, hh, trips, second_visit):
        firsts = []
        for t, slot in trips:
            rows = grp_rows(t if d == 0 else ngroups - 1 - t)
            v2 = jnp.concatenate([v_ref[0, 2 * hh, rows, :], v_ref[0, 2 * hh + 1, rows, :]], axis=-1)
            p = _nt(qd_ref[slot, hh, d], kk_ref[slot, hh, d])
            up = _nn(vt_ref[slot, hh, d], ks_ref[slot, hh, d])
            firsts.append((rows, v2, p, up, qs_ref[slot, hh, d], gdec_ref[slot, hh, d][0:1]))
        yield
        st = st_ref[hh, d]
        outs = []
        for rows, v2, p, up, qs, gdec in firsts:
            a = jnp.where(mask_d[d], p[:, :R], jnp.where(mask_o[d], p[:, R:], 0.0)).astype(_bf16)
            outs.append((rows, _nn(a, v2) + _nt(qs, st.astype(_bf16))))
            st = st * gdec + up
        st_ref[hh, d] = st
        yield
        for rows, o in outs:
            if second_visit:
                finish(hh, rows, o)
            else:
                oacc_ref[hh, rows, :] = o

    def run_staged(tasks):
        tasks = list(tasks)
        while tasks:
            alive = []
            for task in tasks:
                try:
                    next(task)
                    alive.append(task)
                except StopIteration:
                    pass
            tasks = alive

    def per_trip(fn, pair_idx):
        tasks = []
        for slot in range(2):
            t = 2 * pair_idx + slot
            if t < ngroups:
                tasks += [fn(0, t, slot), fn(1, ngroups - 1 - t, slot)]
        return tasks

    st_ref[...] = jnp.zeros_like(st_ref)
    run_staged(per_trip(gates, 0))
    run_staged(per_trip(operands, 0))
    run_staged(per_trip(gates, 1))

    for tt in range(ngroups // 2):
        second_visit = tt >= ngroups // 4
        trips = [(2 * tt, 0), (2 * tt + 1, 1)]
        rec = [recurrence(d, hh, trips, second_visit) for d in range(2) for hh in range(GLA_NH)]
        run_staged(per_trip(gates, tt + 2) + rec + per_trip(operands, tt + 1))


def _gla(proj, lr, wgf, wgb, bgf, bgb, gla_norm, layer):
    B, _, L, _ = proj.shape
    ngroups = L // GROUP_ROWS
    assert ngroups % 4 == 0 and HB % GLA_NH == 0
    nh = GLA_NH
    grid = (B, HB // nh)
    slots = (2, nh, 2)
    return pl.pallas_call(
        functools.partial(_gla_kernel, ngroups=ngroups),
        grid=grid,
        in_specs=[
            pl.BlockSpec((1, nh, L, LANE), lambda b, h: (b, CB_QB // nh + h, 0, 0)),
            pl.BlockSpec((1, nh, L, LANE), lambda b, h: (b, CB_KB // nh + h, 0, 0)),
            pl.BlockSpec((1, 2 * nh, L, LANE), lambda b, h: (b, CB_VB // (2 * nh) + h, 0, 0)),
            pl.BlockSpec((1, 2 * nh, L, LANE), lambda b, h: (b, CB_ZB // (2 * nh) + h, 0, 0)),
            pl.BlockSpec((1, L, 2 * GATE_RANK), lambda b, h: (b, 0, 0)),
            pl.BlockSpec((1, 2 * GATE_RANK, nh * DK), lambda b, h: (layer, 0, h)),
            pl.BlockSpec((1, 2 * GATE_RANK, nh * DK), lambda b, h: (layer, 0, h)),
            pl.BlockSpec((1, 1, nh * DK), lambda b, h: (layer, 0, h)),
            pl.BlockSpec((1, 1, nh * DK), lambda b, h: (layer, 0, h)),
            pl.BlockSpec((1, 1, DV), lambda b, h: (layer, 0, 0)),
        ],
        out_specs=pl.BlockSpec((1, 2 * nh, L, LANE), lambda b, h: (b, h, 0, 0)),
        out_shape=jax.ShapeDtypeStruct((B, 2 * HB, L, LANE), _bf16),
        scratch_shapes=[
            pltpu.VMEM(slots + (GROUP_ROWS, DK), _bf16),
            pltpu.VMEM(slots + (2 * GROUP_ROWS, DK), _bf16),
            pltpu.VMEM(slots + (GROUP_ROWS, DK), _bf16),
            pltpu.VMEM(slots + (GROUP_ROWS, DK), _bf16),
            pltpu.VMEM(slots + (8, DK), _f32),
            pltpu.VMEM(slots + (DV, GROUP_ROWS), _bf16),
            pltpu.VMEM((2, 2, GROUP_ROWS, nh * DK), _f32),
            pltpu.VMEM((2, 2, GROUP_ROWS, nh * DK), _f32),
            pltpu.VMEM((nh, 2, DV, DK), _f32),
            pltpu.VMEM((nh, L, DV), _f32),
        ],
        compiler_params=pltpu.CompilerParams(
            dimension_semantics=("parallel", "parallel"),
            vmem_limit_bytes=VMEM_LIMIT),
        name="gla",
    )(proj, proj, proj, proj, lr, wgf, wgb, bgf, bgb, gla_norm)


OUT_TM = 512


def _out_proj_kernel(a_ref, g_ref, w_ref, x_ref, nw_ref, o_ref):
    mix = jnp.concatenate([a_ref[0, c] for c in range(HA)] + [g_ref[0, c] for c in range(2 * HB)], axis=-1)
    y = _nn(mix, w_ref[0])
    ms = jnp.mean(y * y, axis=-1, keepdims=True)
    o_ref[0] = x_ref[0] + y * lax.rsqrt(ms + EPS) * nw_ref[0]


def _out_proj(attn, gla, w_out, x, norm_w, layer):
    B, L, _ = x.shape
    grid = (B, L // OUT_TM)
    return pl.pallas_call(
        _out_proj_kernel,
        grid=grid,
        in_specs=[
            pl.BlockSpec((1, HA, OUT_TM, LANE), lambda b, i: (b, 0, i, 0)),
            pl.BlockSpec((1, 2 * HB, OUT_TM, LANE), lambda b, i: (b, 0, i, 0)),
            pl.BlockSpec((1, D_MODEL, D_MODEL), lambda b, i: (layer, 0, 0), pipeline_mode=pl.Buffered(1)),
            pl.BlockSpec((1, OUT_TM, D_MODEL), lambda b, i: (b, i, 0)),
            pl.BlockSpec((1, 1, D_MODEL), lambda b, i: (layer, 0, 0)),
        ],
        out_specs=pl.BlockSpec((1, OUT_TM, D_MODEL), lambda b, i: (b, i, 0)),
        out_shape=jax.ShapeDtypeStruct((B, L, D_MODEL), _f32),
        compiler_params=pltpu.CompilerParams(
            dimension_semantics=("parallel", "parallel"),
            vmem_limit_bytes=VMEM_LIMIT),
        name="out_proj",
    )(attn, gla, w_out, x, norm_w)


def _band_tables():
    qi = np.arange(BLOCK)[:, None]
    kj = np.arange(3 * BLOCK)[None, :]
    rel = kj - BLOCK - qi
    nb = N_BUCKETS // 2
    max_exact = nb // 2
    n = np.abs(rel)
    large = max_exact + (np.log(np.maximum(n, 1) / max_exact) / np.log(MAX_DIST / max_exact)
                         * (nb - max_exact)).astype(np.int32)
    large = np.minimum(large, nb - 1)
    bucket = (rel > 0).astype(np.int32) * nb + np.where(n < max_exact, n, large)
    return bucket.astype(np.int32), (n <= WINDOW)


def _position_bias(rel_bias):
    bucket, in_band = _band_tables()
    onehot = jnp.asarray(np.eye(N_BUCKETS, dtype=np.float32)[bucket])
    pos_bias = jnp.einsum("qkn,nh->hqk", onehot, rel_bias.astype(_f32), precision=lax.Precision.HIGHEST)
    key_block = np.arange(3 * BLOCK)[None, :] // BLOCK
    tables = [jnp.where((in_band & keep)[None], pos_bias * LOG2E, NEG)
              for keep in (key_block >= 0, key_block >= 1, key_block <= 1)]
    return jnp.stack(tables, axis=0)


def _gate_weights(w_gk_fwd, w_gk_bwd):
    zeros = jnp.zeros_like(w_gk_fwd)
    wgf = jnp.concatenate([w_gk_fwd, zeros], axis=1).astype(_bf16)
    wgb = jnp.concatenate([zeros, w_gk_bwd], axis=1).astype(_bf16)
    return wgf, wgb


def _layer(x, layer, pos_bias, w_in, wgf, wgb, bgf, bgb, sink, gla_norm, w_out, norm_pre, norm_post):
    proj, lr = _in_proj(x, norm_pre, w_in, layer)
    attn = _attention(proj, pos_bias, sink[layer])
    gla = _gla(proj, lr, wgf, wgb, bgf, bgb, gla_norm, layer)
    return _out_proj(attn, gla, w_out, x, norm_post, layer)


def _prepare(rel_bias, w_in, w_gk_fwd, b_gk_fwd, w_gk_bwd, b_gk_bwd, sink, gla_norm, w_out, norm_pre, norm_post):
    wgf, wgb = _gate_weights(w_gk_fwd, w_gk_bwd)
    col_scale = np.ones((w_in.shape[2],), np.float32)
    col_scale[CB_QA * LANE:CB_QA * LANE + WA] = DH ** -0.5 * LOG2E
    col_scale[CB_QB * LANE:CB_QB * LANE + WBK] = DK ** -0.5
    w_in = jnp.swapaxes(w_in * col_scale, 1, 2).astype(_bf16)
    return (_position_bias(rel_bias), w_in, wgf, wgb, b_gk_fwd[:, None, :], b_gk_bwd[:, None, :],
            sink, gla_norm[:, None, :], w_out.astype(_bf16), norm_pre[:, None, :], norm_post[:, None, :])


def kernel(x_prompt, x_sample, rel_bias, w_in, w_gk_fwd, b_gk_fwd, w_gk_bwd, b_gk_bwd, sink, gla_norm, w_out, norm_pre, norm_post):
    params = _prepare(rel_bias, w_in, w_gk_fwd, b_gk_fwd, w_gk_bwd, b_gk_bwd, sink, gla_norm, w_out,
                      norm_pre, norm_post)
    y_prompt, y_sample = x_prompt, x_sample
    for l in range(w_in.shape[0]):
        y_prompt = _layer(y_prompt, l, *params)
        y_sample = _layer(y_sample, l, *params)
    return (y_prompt, y_sample)
```

```python
import functools

import jax
import jax.numpy as jnp
import numpy as np
from jax import lax
from jax.experimental import pallas as pl
from jax.experimental.pallas import tpu as pltpu

D_MODEL = 2048
HA, HKV, DH = 8, 2, 128
GROUP = HA // HKV
WINDOW, BLOCK = 128, 128
N_BUCKETS, MAX_DIST = 32, 128
HB, DK, DV = 4, 128, 256
GATE_RANK, GATE_NORM = 16, 16.0
CHUNK = 64
NEG = -1e30
EPS = 1e-6

LANE = 128
GROUP_ROWS = 2 * CHUNK

WA = HA * DH
WKV = HKV * DH
WBK = HB * DK
WB = HB * DV
N_MAIN = 2 * WA + 2 * WKV + 2 * WBK + 2 * WB
CB_QA = 0
CB_KA = CB_QA + WA // LANE
CB_VA = CB_KA + WKV // LANE
CB_ZA = CB_VA + WKV // LANE
CB_QB = CB_ZA + WA // LANE
CB_KB = CB_QB + WBK // LANE
CB_VB = CB_KB + WBK // LANE
CB_ZB = CB_VB + WB // LANE
N_CB = N_MAIN // LANE
N_GATE_CB = 2 * WBK // LANE

VMEM_LIMIT = 56 * 1024 * 1024

LOG2E = 1.4426950408889634

_f32 = jnp.float32
_bf16 = jnp.bfloat16


def _nt(a, b):
    return lax.dot_general(a, b, (((1,), (1,)), ((), ())), preferred_element_type=_f32)


def _nn(a, b):
    return jnp.dot(a, b, preferred_element_type=_f32)


def _silu(z):
    h = 0.5 * z
    return h + h * jnp.tanh(h)


def _log2_sigmoid_scaled(x, s):
    return jnp.minimum(x, 0.0) * (s * LOG2E) - jnp.log2(1.0 + jnp.exp2(jnp.abs(x) * (-LOG2E))) * s


IN_TH = 256
IN_TN = 512
IN_RC = 128


def _in_proj_kernel(x0_ref, xa_ref, xb_ref, nw_ref, w_ref, wg_ref, bg_ref, o_ref, g_ref, xn_ref):
    g = pl.program_id(0)
    nw = nw_ref[0]

    def normalise(x_ref, slot):
        for r in range(IN_TH // IN_RC):
            rows = slice(r * IN_RC, (r + 1) * IN_RC)
            x = x_ref[0, rows, :]
            ms = jnp.mean(x * x, axis=-1, keepdims=True)
            xn_ref[slot, rows, :] = (x * lax.rsqrt(ms + EPS) * nw).astype(_bf16)

    def project(slot):
        rows = slice(slot * IN_TH, (slot + 1) * IN_TH)
        xp = xn_ref[slot]
        lr = _nt(xp, w_ref[0, N_MAIN:N_MAIN + 2 * GATE_RANK, :]).astype(_bf16)
        for j in range(N_MAIN // IN_TN):
            acc = _nt(xp, w_ref[0, j * IN_TN:(j + 1) * IN_TN, :])
            for c in range(IN_TN // LANE):
                o_ref[0, j * (IN_TN // LANE) + c, rows, :] = acc[:, c * LANE:(c + 1) * LANE].astype(_bf16)
            if j == 0:
                gates = _log2_sigmoid_scaled(_nn(lr, wg_ref[0]) + bg_ref[0], 1.0 / GATE_NORM).astype(_bf16)
                for c in range(N_GATE_CB):
                    g_ref[0, c, rows, :] = gates[:, c * LANE:(c + 1) * LANE]

    @pl.when(g == 0)
    def _():
        normalise(x0_ref, 0)

    normalise(xa_ref, 1)
    project(0)
    normalise(xb_ref, 0)
    project(1)


def _in_proj(x, norm_w, w_in, wg, bg, layer):
    B, L, _ = x.shape
    nh = L // IN_TH
    halves = B * nh
    steps = halves // 2

    def half_map(offset):
        def index(g):
            t = jnp.minimum(2 * g + offset, halves - 1)
            return (t // nh, t % nh, 0)
        return index

    def out_map(g):
        return (g // (nh // 2), 0, g % (nh // 2), 0)

    half_spec = (1, IN_TH, D_MODEL)
    return pl.pallas_call(
        _in_proj_kernel,
        grid=(steps,),
        in_specs=[
            pl.BlockSpec(half_spec, lambda g: (0, 0, 0)),
            pl.BlockSpec(half_spec, half_map(1)),
            pl.BlockSpec(half_spec, half_map(2)),
            pl.BlockSpec((1, 1, D_MODEL), lambda g: (layer, 0, 0)),
            pl.BlockSpec((1, w_in.shape[1], D_MODEL), lambda g: (layer, 0, 0), pipeline_mode=pl.Buffered(1)),
            pl.BlockSpec((1, 2 * GATE_RANK, 2 * WBK), lambda g: (layer, 0, 0)),
            pl.BlockSpec((1, 1, 2 * WBK), lambda g: (layer, 0, 0)),
        ],
        out_specs=[
            pl.BlockSpec((1, N_CB, 2 * IN_TH, LANE), out_map),
            pl.BlockSpec((1, N_GATE_CB, 2 * IN_TH, LANE), out_map),
        ],
        out_shape=[
            jax.ShapeDtypeStruct((B, N_CB, L, LANE), _bf16),
            jax.ShapeDtypeStruct((B, N_GATE_CB, L, LANE), _bf16),
        ],
        scratch_shapes=[pltpu.VMEM((2, IN_TH, D_MODEL), _bf16)],
        compiler_params=pltpu.CompilerParams(
            dimension_semantics=("arbitrary",),
            vmem_limit_bytes=VMEM_LIMIT),
        name="in_proj",
    )(x, x, x, norm_w, w_in, wg, bg)


def _attn_kernel(sink_ref, q_ref, k_ref, v_ref, z_ref, bias_ref, o_ref, s_ref, p_ref, t_ref, *, nblk):
    h = pl.program_id(1)

    def blk_rows(j):
        return slice(j * BLOCK, (j + 1) * BLOCK)

    def window(ref, blk):
        lb = max(blk - 1, 0)
        rb = min(blk + 1, nblk - 1)
        return jnp.concatenate([ref[0, 0, blk_rows(lb), :], ref[0, 0, blk_rows(blk), :],
                                ref[0, 0, blk_rows(rb), :]], axis=0)

    def stage_a(blk, slot):
        q4 = q_ref[0, :, blk_rows(blk), :].reshape(GROUP * BLOCK, DH)
        s_ref[slot] = _nt(q4, window(k_ref, blk))

    def stage_b(blk, src, dst):
        table = 1 if blk == 0 else (2 if blk == nblk - 1 else 0)
        for g in range(GROUP):
            gr = slice(g * BLOCK, (g + 1) * BLOCK)
            sg = s_ref[src, gr, :] + bias_ref[table, g]
            sk = sink_ref[h * GROUP + g] * LOG2E
            m = jnp.maximum(jnp.max(sg, axis=-1, keepdims=True), sk)
            p_ref[dst, gr, :] = jnp.exp2(sg - m).astype(_bf16)
            t_ref[dst, g] = jnp.broadcast_to(jnp.exp2(sk - m), (BLOCK, DH))

    def stage_c(blk, src):
        vw = window(v_ref, blk)
        vw = jnp.concatenate([vw, jnp.ones_like(vw)], axis=-1)
        o4 = _nn(p_ref[src], vw)
        rows = blk_rows(blk)
        for g in range(GROUP):
            og = o4[g * BLOCK:(g + 1) * BLOCK]
            z = z_ref[0, g, rows, :].astype(_f32)
            den = og[:, DH:] + t_ref[src, g]
            o_ref[0, g, rows, :] = (og[:, :DH] * (1.0 / den) * _silu(z)).astype(_bf16)

    for it in range(nblk + 2):
        par = it % 2
        if 0 <= it - 2 < nblk:
            stage_c(it - 2, 1 - par)
        if 0 <= it - 1 < nblk:
            stage_b(it - 1, 1 - par, par)
        if it < nblk:
            stage_a(it, par)


def _attention(proj, pos_bias, sink):
    B, _, L, _ = proj.shape
    nblk = L // BLOCK
    assert nblk >= 2 and nblk % 2 == 0
    grid = (B, HKV)
    return pl.pallas_call(
        functools.partial(_attn_kernel, nblk=nblk),
        grid=grid,
        in_specs=[
            pl.BlockSpec(memory_space=pltpu.SMEM),
            pl.BlockSpec((1, GROUP, L, LANE), lambda b, h: (b, CB_QA // GROUP + h, 0, 0)),
            pl.BlockSpec((1, 1, L, LANE), lambda b, h: (b, CB_KA + h, 0, 0)),
            pl.BlockSpec((1, 1, L, LANE), lambda b, h: (b, CB_VA + h, 0, 0)),
            pl.BlockSpec((1, GROUP, L, LANE), lambda b, h: (b, CB_ZA // GROUP + h, 0, 0)),
            pl.BlockSpec((3, GROUP, BLOCK, 3 * BLOCK), lambda b, h: (0, h, 0, 0)),
        ],
        out_specs=pl.BlockSpec((1, GROUP, L, LANE), lambda b, h: (b, h, 0, 0)),
        out_shape=jax.ShapeDtypeStruct((B, HA, L, LANE), _bf16),
        scratch_shapes=[
            pltpu.VMEM((2, GROUP * BLOCK, 3 * BLOCK), _f32),
            pltpu.VMEM((2, GROUP * BLOCK, 3 * BLOCK), _bf16),
            pltpu.VMEM((2, GROUP, BLOCK, DH), _f32),
        ],
        compiler_params=pltpu.CompilerParams(
            dimension_semantics=("parallel", "parallel"),
            vmem_limit_bytes=VMEM_LIMIT),
        name="attention",
    )(sink, proj, proj, proj, proj, pos_bias)


GLA_NH = 2


def _gla_kernel(q_ref, k_ref, v_ref, z_ref, gf_ref, gb_ref, gn_ref,
                o_ref,
                qd_ref, kk_ref, qs_ref, ks_ref, gdec_ref, vt_ref, pre_ref, st_ref, oacc_ref, *, ngroups):
    R = GROUP_ROWS
    row = lax.broadcasted_iota(jnp.int32, (R, R), 0)
    colm = lax.broadcasted_iota(jnp.int32, (R, R), 1)
    same_chunk = (row >= CHUNK) == (colm >= CHUNK)
    tri = jnp.where(same_chunk & (colm <= row), 1.0, 0.0).astype(_bf16)
    mask_d = (same_chunk & (colm <= row), same_chunk & (colm >= row))
    mask_o = ((row >= CHUNK) & (colm < CHUNK), (row < CHUNK) & (colm >= CHUNK))
    g_refs = (gf_ref, gb_ref)

    def grp_rows(g):
        return slice(g * R, (g + 1) * R)

    def log_gates(d, rows):
        return jnp.concatenate([g_refs[d][0, hh, rows, :] for hh in range(GLA_NH)], axis=-1)

    def rep(x):
        return jnp.broadcast_to(x, (CHUNK, x.shape[-1]))

    def prefix(d, g, slot):
        pre = _nn(tri, log_gates(d, grp_rows(g)))
        yield
        yield
        pre_ref[slot, d] = pre

    def operands(d, g, slot):
        yield
        rows = grp_rows(g)
        pre = pre_ref[slot, d]
        t0, t1 = pre[CHUNK - 1:CHUNK], pre[R - 1:R]
        e0, e1 = jnp.exp2(t0), jnp.exp2(t1)
        tot = jnp.concatenate([rep(t0), rep(t1)], axis=0)
        b = pre if d == 0 else tot - pre + log_gates(d, rows).astype(_f32)
        eb, enb = jnp.exp2(b), jnp.exp2(-b)
        etot = jnp.concatenate([rep(e0), rep(e1)], axis=0)
        for hh in range(GLA_NH):
            sl = slice(hh * DK, (hh + 1) * DK)
            q = q_ref[0, hh, rows, :].astype(_f32)
            k = k_ref[0, hh, rows, :].astype(_f32)
            qd = q * eb[:, sl]
            kd = k * enb[:, sl]
            kt = kd * etot[:, sl]
            if d == 0:
                qs = jnp.concatenate([qd[:CHUNK], qd[CHUNK:] * e0[:, sl]], axis=0)
                ks = jnp.concatenate([kt[:CHUNK] * e1[:, sl], kt[CHUNK:]], axis=0)
            else:
                qs = jnp.concatenate([qd[:CHUNK] * e1[:, sl], qd[CHUNK:]], axis=0)
                ks = jnp.concatenate([kt[:CHUNK], kt[CHUNK:] * e0[:, sl]], axis=0)
            qd_ref[slot, hh, d] = qd.astype(_bf16)
            kk_ref[slot, hh, d, 0:R, :] = kd.astype(_bf16)
            kk_ref[slot, hh, d, R:2 * R, :] = kt.astype(_bf16)
            qs_ref[slot, hh, d] = qs.astype(_bf16)
            ks_ref[slot, hh, d] = ks.astype(_bf16)
            gdec_ref[slot, hh, d] = jnp.broadcast_to((e0 * e1)[:, sl], (8, DK))
            v2 = jnp.concatenate([v_ref[0, 2 * hh, rows, :], v_ref[0, 2 * hh + 1, rows, :]], axis=-1)
            vt_ref[slot, hh, d] = v2.T

    gn = gn_ref[0]

    def finish(hh, rows, o):
        o = o + oacc_ref[hh, rows, :]
        ms = jnp.mean(o * o, axis=-1, keepdims=True)
        y = o * lax.rsqrt(ms + EPS) * gn
        z = jnp.concatenate([z_ref[0, 2 * hh, rows, :], z_ref[0, 2 * hh + 1, rows, :]], axis=-1).astype(_f32)
        y = (y * _silu(z)).astype(_bf16)
        o_ref[0, 2 * hh, rows, :] = y[:, :LANE]
        o_ref[0, 2 * hh + 1, rows, :] = y[:, LANE:]

    def recurrence(d, hh, trips, second_visit):
        firsts = []
        for t, slot in trips:
            rows = grp_rows(t if d == 0 else ngroups - 1 - t)
            v2 = jnp.concatenate([v_ref[0, 2 * hh, rows, :], v_ref[0, 2 * hh + 1, rows, :]], axis=-1)
            p = _nt(qd_ref[slot, hh, d], kk_ref[slot, hh, d])
            up = _nn(vt_ref[slot, hh, d], ks_ref[slot, hh, d])
            firsts.append((rows, v2, p, up, qs_ref[slot, hh, d], gdec_ref[slot, hh, d][0:1]))
        yield
        st = st_ref[hh, d]
        outs = []
        for rows, v2, p, up, qs, gdec in firsts:
            a = jnp.where(mask_d[d], p[:, :R], jnp.where(mask_o[d], p[:, R:], 0.0)).astype(_bf16)
            outs.append((rows, _nn(a, v2) + _nt(qs, st.astype(_bf16))))
            st = st * gdec + up
        st_ref[hh, d] = st
        yield
        for rows, o in outs:
            if second_visit:
                finish(hh, rows, o)
            else:
                oacc_ref[hh, rows, :] = o

    def run_staged(tasks):
        tasks = list(tasks)
        while tasks:
            alive = []
            for task in tasks:
                try:
                    next(task)
                    alive.append(task)
                except StopIteration:
                    pass
            tasks = alive

    def per_trip(fn, pair_idx):
        tasks = []
        for slot in range(2):
            t = 2 * pair_idx + slot
            if t < ngroups:
                tasks += [fn(0, t, slot), fn(1, ngroups - 1 - t, slot)]
        return tasks

    st_ref[...] = jnp.zeros_like(st_ref)
    run_staged(per_trip(prefix, 0))
    run_staged(per_trip(operands, 0))
    run_staged(per_trip(prefix, 1))

    for tt in range(ngroups // 2):
        second_visit = tt >= ngroups // 4
        trips = [(2 * tt, 0), (2 * tt + 1, 1)]
        rec = [recurrence(d, hh, trips, second_visit) for d in range(2) for hh in range(GLA_NH)]
        run_staged(per_trip(prefix, tt + 2) + rec + per_trip(operands, tt + 1))


def _gla(proj, gates, gla_norm, layer):
    B, _, L, _ = proj.shape
    ngroups = L // GROUP_ROWS
    assert ngroups % 4 == 0 and HB % GLA_NH == 0
    nh = GLA_NH
    grid = (B, HB // nh)
    slots = (2, nh, 2)
    return pl.pallas_call(
        functools.partial(_gla_kernel, ngroups=ngroups),
        grid=grid,
        in_specs=[
            pl.BlockSpec((1, nh, L, LANE), lambda b, h: (b, CB_QB // nh + h, 0, 0)),
            pl.BlockSpec((1, nh, L, LANE), lambda b, h: (b, CB_KB // nh + h, 0, 0)),
            pl.BlockSpec((1, 2 * nh, L, LANE), lambda b, h: (b, CB_VB // (2 * nh) + h, 0, 0)),
            pl.BlockSpec((1, 2 * nh, L, LANE), lambda b, h: (b, CB_ZB // (2 * nh) + h, 0, 0)),
            pl.BlockSpec((1, nh, L, LANE), lambda b, h: (b, h, 0, 0)),
            pl.BlockSpec((1, nh, L, LANE), lambda b, h: (b, HB // nh + h, 0, 0)),
            pl.BlockSpec((1, 1, DV), lambda b, h: (layer, 0, 0)),
        ],
        out_specs=pl.BlockSpec((1, 2 * nh, L, LANE), lambda b, h: (b, h, 0, 0)),
        out_shape=jax.ShapeDtypeStruct((B, 2 * HB, L, LANE), _bf16),
        scratch_shapes=[
            pltpu.VMEM(slots + (GROUP_ROWS, DK), _bf16),
            pltpu.VMEM(slots + (2 * GROUP_ROWS, DK), _bf16),
            pltpu.VMEM(slots + (GROUP_ROWS, DK), _bf16),
            pltpu.VMEM(slots + (GROUP_ROWS, DK), _bf16),
            pltpu.VMEM(slots + (8, DK), _f32),
            pltpu.VMEM(slots + (DV, GROUP_ROWS), _bf16),
            pltpu.VMEM((2, 2, GROUP_ROWS, nh * DK), _f32),
            pltpu.VMEM((nh, 2, DV, DK), _f32),
            pltpu.VMEM((nh, L, DV), _f32),
        ],
        compiler_params=pltpu.CompilerParams(
            dimension_semantics=("parallel", "parallel"),
            vmem_limit_bytes=VMEM_LIMIT),
        name="gla",
    )(proj, proj, proj, proj, gates, gates, gla_norm)


OUT_TM = 512


def _out_proj_kernel(a_ref, g_ref, w_ref, x_ref, nw_ref, o_ref):
    mix = jnp.concatenate([a_ref[0, c] for c in range(HA)] + [g_ref[0, c] for c in range(2 * HB)], axis=-1)
    y = _nn(mix, w_ref[0])
    ms = jnp.mean(y * y, axis=-1, keepdims=True)
    o_ref[0] = x_ref[0] + y * lax.rsqrt(ms + EPS) * nw_ref[0]


def _out_proj(attn, gla, w_out, x, norm_w, layer):
    B, L, _ = x.shape
    grid = (B, L // OUT_TM)
    return pl.pallas_call(
        _out_proj_kernel,
        grid=grid,
        in_specs=[
            pl.BlockSpec((1, HA, OUT_TM, LANE), lambda b, i: (b, 0, i, 0)),
            pl.BlockSpec((1, 2 * HB, OUT_TM, LANE), lambda b, i: (b, 0, i, 0)),
            pl.BlockSpec((1, D_MODEL, D_MODEL), lambda b, i: (layer, 0, 0), pipeline_mode=pl.Buffered(1)),
            pl.BlockSpec((1, OUT_TM, D_MODEL), lambda b, i: (b, i, 0)),
            pl.BlockSpec((1, 1, D_MODEL), lambda b, i: (layer, 0, 0)),
        ],
        out_specs=pl.BlockSpec((1, OUT_TM, D_MODEL), lambda b, i: (b, i, 0)),
        out_shape=jax.ShapeDtypeStruct((B, L, D_MODEL), _f32),
        compiler_params=pltpu.CompilerParams(
            dimension_semantics=("parallel", "parallel"),
            vmem_limit_bytes=VMEM_LIMIT),
        name="out_proj",
    )(attn, gla, w_out, x, norm_w)


def _band_tables():
    qi = np.arange(BLOCK)[:, None]
    kj = np.arange(3 * BLOCK)[None, :]
    rel = kj - BLOCK - qi
    nb = N_BUCKETS // 2
    max_exact = nb // 2
    n = np.abs(rel)
    large = max_exact + (np.log(np.maximum(n, 1) / max_exact) / np.log(MAX_DIST / max_exact)
                         * (nb - max_exact)).astype(np.int32)
    large = np.minimum(large, nb - 1)
    bucket = (rel > 0).astype(np.int32) * nb + np.where(n < max_exact, n, large)
    return bucket.astype(np.int32), (n <= WINDOW)


def _position_bias(rel_bias):
    bucket, in_band = _band_tables()
    onehot = jnp.asarray(np.eye(N_BUCKETS, dtype=np.float32)[bucket])
    pos_bias = jnp.einsum("qkn,nh->hqk", onehot, rel_bias.astype(_f32), precision=lax.Precision.HIGHEST)
    key_block = np.arange(3 * BLOCK)[None, :] // BLOCK
    tables = [jnp.where((in_band & keep)[None], pos_bias * LOG2E, NEG)
              for keep in (key_block >= 0, key_block >= 1, key_block <= 1)]
    return jnp.stack(tables, axis=0)


def _gate_weights(w_gk_fwd, w_gk_bwd):
    zeros = jnp.zeros_like(w_gk_fwd)
    return jnp.concatenate([jnp.concatenate([w_gk_fwd, zeros], axis=1),
                            jnp.concatenate([zeros, w_gk_bwd], axis=1)], axis=2).astype(_bf16)


def _layer(x, layer, pos_bias, w_in, wg, bg, sink, gla_norm, w_out, norm_pre, norm_post):
    proj, gates = _in_proj(x, norm_pre, w_in, wg, bg, layer)
    attn = _attention(proj, pos_bias, sink[layer])
    gla = _gla(proj, gates, gla_norm, layer)
    return _out_proj(attn, gla, w_out, x, norm_post, layer)


def _prepare(rel_bias, w_in, w_gk_fwd, b_gk_fwd, w_gk_bwd, b_gk_bwd, sink, gla_norm, w_out, norm_pre, norm_post):
    wg = _gate_weights(w_gk_fwd, w_gk_bwd)
    bg = jnp.concatenate([b_gk_fwd, b_gk_bwd], axis=-1)[:, None, :]
    col_scale = np.ones((w_in.shape[2],), np.float32)
    col_scale[CB_QA * LANE:CB_QA * LANE + WA] = DH ** -0.5 * LOG2E
    col_scale[CB_QB * LANE:CB_QB * LANE + WBK] = DK ** -0.5
    w_in = jnp.swapaxes(w_in * col_scale, 1, 2).astype(_bf16)
    return (_position_bias(rel_bias), w_in, wg, bg,
            sink, gla_norm[:, None, :], w_out.astype(_bf16), norm_pre[:, None, :], norm_post[:, None, :])


def kernel(x_prompt, x_sample, rel_bias, w_in, w_gk_fwd, b_gk_fwd, w_gk_bwd, b_gk_bwd, sink, gla_norm, w_out, norm_pre, norm_post):
    params = _prepare(rel_bias, w_in, w_gk_fwd, b_gk_fwd, w_gk_bwd, b_gk_bwd, sink, gla_norm, w_out,
                      norm_pre, norm_post)
    y_prompt, y_sample = x_prompt, x_sample
    for l in range(w_in.shape[0]):
        y_prompt = _layer(y_prompt, l, *params)
        y_sample = _layer(y_sample, l, *params)
    return (y_prompt, y_sample)
```

```python
import functools

import jax
import jax.numpy as jnp
import numpy as np
from jax import lax
from jax.experimental import pallas as pl
from jax.experimental.pallas import tpu as pltpu

D_MODEL = 2048
HA, HKV, DH = 8, 2, 128
GROUP = HA // HKV
WINDOW, BLOCK = 128, 128
N_BUCKETS, MAX_DIST = 32, 128
HB, DK, DV = 4, 128, 256
GATE_RANK, GATE_NORM = 16, 16.0
CHUNK = 64
NEG = -1e30
EPS = 1e-6

LANE = 128
GROUP_ROWS = 2 * CHUNK

WA = HA * DH
WKV = HKV * DH
WBK = HB * DK
WB = HB * DV
N_MAIN = 2 * WA + 2 * WKV + 2 * WBK + 2 * WB
CB_QA = 0
CB_KA = CB_QA + WA // LANE
CB_VA = CB_KA + WKV // LANE
CB_ZA = CB_VA + WKV // LANE
CB_QB = CB_ZA + WA // LANE
CB_KB = CB_QB + WBK // LANE
CB_VB = CB_KB + WBK // LANE
CB_ZB = CB_VB + WB // LANE
N_CB = N_MAIN // LANE
N_GATE_CB = 2 * WBK // LANE
GATE_TILES = tuple(range(CB_ZA, CB_QB, 4)) + tuple(range(CB_ZB, N_CB, 4))

VMEM_LIMIT = 56 * 1024 * 1024

LOG2E = 1.4426950408889634

_f32 = jnp.float32
_bf16 = jnp.bfloat16


def _nt(a, b):
    return lax.dot_general(a, b, (((1,), (1,)), ((), ())), preferred_element_type=_f32)


def _nn(a, b):
    return jnp.dot(a, b, preferred_element_type=_f32)


def _silu(z):
    h = 0.5 * z
    return h + h * jnp.tanh(h)


def _log2_sigmoid_scaled(x, s):
    return jnp.minimum(x, 0.0) * (s * LOG2E) - jnp.log2(1.0 + jnp.exp2(jnp.abs(x) * (-LOG2E))) * s


IN_TH = 256
IN_TN = 512
IN_RC = 128


def _in_proj_kernel(x0_ref, xa_ref, xb_ref, nw_ref, w_ref, wg_ref, bg_ref, o_ref, g_ref, xn_ref):
    g = pl.program_id(0)
    nw = nw_ref[0]

    def normalise(x_ref, slot):
        for r in range(IN_TH // IN_RC):
            rows = slice(r * IN_RC, (r + 1) * IN_RC)
            x = x_ref[0, rows, :]
            ms = jnp.mean(x * x, axis=-1, keepdims=True)
            xn_ref[slot, rows, :] = (x * lax.rsqrt(ms + EPS) * nw).astype(_bf16)

    def project(slot):
        rows = slice(slot * IN_TH, (slot + 1) * IN_TH)
        xp = xn_ref[slot]
        lr = _nt(xp, w_ref[0, N_MAIN:N_MAIN + 2 * GATE_RANK, :]).astype(_bf16)
        for j in range(N_MAIN // IN_TN):
            acc = _nt(xp, w_ref[0, j * IN_TN:(j + 1) * IN_TN, :])
            if j * IN_TN // LANE in GATE_TILES:
                acc = _silu(acc)
            for c in range(IN_TN // LANE):
                o_ref[0, j * (IN_TN // LANE) + c, rows, :] = acc[:, c * LANE:(c + 1) * LANE].astype(_bf16)
            if j == 0:
                gates = _log2_sigmoid_scaled(_nn(lr, wg_ref[0]) + bg_ref[0], 1.0 / GATE_NORM).astype(_bf16)
                for c in range(N_GATE_CB):
                    g_ref[0, c, rows, :] = gates[:, c * LANE:(c + 1) * LANE]

    @pl.when(g == 0)
    def _():
        normalise(x0_ref, 0)

    normalise(xa_ref, 1)
    project(0)
    normalise(xb_ref, 0)
    project(1)


def _in_proj(x, norm_w, w_in, wg, bg, layer):
    B, L, _ = x.shape
    nh = L // IN_TH
    halves = B * nh
    steps = halves // 2

    def half_map(offset):
        def index(g):
            t = jnp.minimum(2 * g + offset, halves - 1)
            return (t // nh, t % nh, 0)
        return index

    def out_map(g):
        return (g // (nh // 2), 0, g % (nh // 2), 0)

    half_spec = (1, IN_TH, D_MODEL)
    return pl.pallas_call(
        _in_proj_kernel,
        grid=(steps,),
        in_specs=[
            pl.BlockSpec(half_spec, lambda g: (0, 0, 0)),
            pl.BlockSpec(half_spec, half_map(1)),
            pl.BlockSpec(half_spec, half_map(2)),
            pl.BlockSpec((1, 1, D_MODEL), lambda g: (layer, 0, 0)),
            pl.BlockSpec((1, w_in.shape[1], D_MODEL), lambda g: (layer, 0, 0), pipeline_mode=pl.Buffered(1)),
            pl.BlockSpec((1, 2 * GATE_RANK, 2 * WBK), lambda g: (layer, 0, 0)),
            pl.BlockSpec((1, 1, 2 * WBK), lambda g: (layer, 0, 0)),
        ],
        out_specs=[
            pl.BlockSpec((1, N_CB, 2 * IN_TH, LANE), out_map),
            pl.BlockSpec((1, N_GATE_CB, 2 * IN_TH, LANE), out_map),
        ],
        out_shape=[
            jax.ShapeDtypeStruct((B, N_CB, L, LANE), _bf16),
            jax.ShapeDtypeStruct((B, N_GATE_CB, L, LANE), _bf16),
        ],
        scratch_shapes=[pltpu.VMEM((2, IN_TH, D_MODEL), _bf16)],
        compiler_params=pltpu.CompilerParams(
            dimension_semantics=("arbitrary",),
            vmem_limit_bytes=VMEM_LIMIT),
        name="in_proj",
    )(x, x, x, norm_w, w_in, wg, bg)


def _attn_kernel(sink_ref, q_ref, k_ref, v_ref, z_ref, bias_ref, o_ref, s_ref, p_ref, t_ref, *, nblk):
    h = pl.program_id(1)

    def blk_rows(j):
        return slice(j * BLOCK, (j + 1) * BLOCK)

    def window(ref, blk):
        lb = max(blk - 1, 0)
        rb = min(blk + 1, nblk - 1)
        return jnp.concatenate([ref[0, 0, blk_rows(lb), :], ref[0, 0, blk_rows(blk), :],
                                ref[0, 0, blk_rows(rb), :]], axis=0)

    def stage_a(blk, slot):
        q4 = q_ref[0, :, blk_rows(blk), :].reshape(GROUP * BLOCK, DH)
        s_ref[slot] = _nt(q4, window(k_ref, blk))

    def stage_b(blk, src, dst):
        table = 1 if blk == 0 else (2 if blk == nblk - 1 else 0)
        for g in range(GROUP):
            gr = slice(g * BLOCK, (g + 1) * BLOCK)
            sg = s_ref[src, gr, :] + bias_ref[table, g]
            sk = sink_ref[h * GROUP + g] * LOG2E
            m = jnp.maximum(jnp.max(sg, axis=-1, keepdims=True), sk)
            p_ref[dst, gr, :] = jnp.exp2(sg - m).astype(_bf16)
            t_ref[dst, g] = jnp.broadcast_to(jnp.exp2(sk - m), (BLOCK, DH))

    def stage_c(blk, src):
        vw = window(v_ref, blk)
        vw = jnp.concatenate([vw, jnp.ones_like(vw)], axis=-1)
        o4 = _nn(p_ref[src], vw)
        rows = blk_rows(blk)
        for g in range(GROUP):
            og = o4[g * BLOCK:(g + 1) * BLOCK]
            z = z_ref[0, g, rows, :].astype(_f32)
            den = og[:, DH:] + t_ref[src, g]
            o_ref[0, g, rows, :] = (og[:, :DH] * (1.0 / den) * z).astype(_bf16)

    for it in range(nblk + 2):
        par = it % 2
        if 0 <= it - 2 < nblk:
            stage_c(it - 2, 1 - par)
        if 0 <= it - 1 < nblk:
            stage_b(it - 1, 1 - par, par)
        if it < nblk:
            stage_a(it, par)


def _attention(proj, pos_bias, sink):
    B, _, L, _ = proj.shape
    nblk = L // BLOCK
    assert nblk >= 2 and nblk % 2 == 0
    grid = (B, HKV)
    return pl.pallas_call(
        functools.partial(_attn_kernel, nblk=nblk),
        grid=grid,
        in_specs=[
            pl.BlockSpec(memory_space=pltpu.SMEM),
            pl.BlockSpec((1, GROUP, L, LANE), lambda b, h: (b, CB_QA // GROUP + h, 0, 0)),
            pl.BlockSpec((1, 1, L, LANE), lambda b, h: (b, CB_KA + h, 0, 0)),
            pl.BlockSpec((1, 1, L, LANE), lambda b, h: (b, CB_VA + h, 0, 0)),
            pl.BlockSpec((1, GROUP, L, LANE), lambda b, h: (b, CB_ZA // GROUP + h, 0, 0)),
            pl.BlockSpec((3, GROUP, BLOCK, 3 * BLOCK), lambda b, h: (0, h, 0, 0)),
        ],
        out_specs=pl.BlockSpec((1, GROUP, L, LANE), lambda b, h: (b, h, 0, 0)),
        out_shape=jax.ShapeDtypeStruct((B, HA, L, LANE), _bf16),
        scratch_shapes=[
            pltpu.VMEM((2, GROUP * BLOCK, 3 * BLOCK), _f32),
            pltpu.VMEM((2, GROUP * BLOCK, 3 * BLOCK), _bf16),
            pltpu.VMEM((2, GROUP, BLOCK, DH), _f32),
        ],
        compiler_params=pltpu.CompilerParams(
            dimension_semantics=("parallel", "parallel"),
            vmem_limit_bytes=VMEM_LIMIT),
        name="attention",
    )(sink, proj, proj, proj, proj, pos_bias)


GLA_NH = 1
GLA_NT = 2


def _gla_kernel(q_ref, k_ref, v_ref, z_ref, gf_ref, gb_ref, gn_ref,
                o_ref,
                qd_ref, kk_ref, qs_ref, ks_ref, gdec_ref, vt_ref, pre_ref, st_ref, oacc_ref, *, ngroups):
    R = GROUP_ROWS
    row = lax.broadcasted_iota(jnp.int32, (R, R), 0)
    colm = lax.broadcasted_iota(jnp.int32, (R, R), 1)
    same_chunk = (row >= CHUNK) == (colm >= CHUNK)
    tri = jnp.where(same_chunk & (colm <= row), 1.0, 0.0).astype(_bf16)
    mask_d = (same_chunk & (colm <= row), same_chunk & (colm >= row))
    mask_o = ((row >= CHUNK) & (colm < CHUNK), (row < CHUNK) & (colm >= CHUNK))
    g_refs = (gf_ref, gb_ref)

    def grp_rows(g):
        return slice(g * R, (g + 1) * R)

    def log_gates(d, rows):
        return jnp.concatenate([g_refs[d][0, hh, rows, :] for hh in range(GLA_NH)], axis=-1)

    def rep(x):
        return jnp.broadcast_to(x, (CHUNK, x.shape[-1]))

    def prefix(d, g, slot):
        pre = _nn(tri, log_gates(d, grp_rows(g)))
        yield
        yield
        pre_ref[slot, d] = pre

    def operands(d, g, slot):
        yield
        rows = grp_rows(g)
        pre = pre_ref[slot, d]
        t0, t1 = pre[CHUNK - 1:CHUNK], pre[R - 1:R]
        e0, e1 = jnp.exp2(t0), jnp.exp2(t1)
        tot = jnp.concatenate([rep(t0), rep(t1)], axis=0)
        b = pre if d == 0 else tot - pre + log_gates(d, rows).astype(_f32)
        eb, enb = jnp.exp2(b), jnp.exp2(-b)
        etot = jnp.concatenate([rep(e0), rep(e1)], axis=0)
        for hh in range(GLA_NH):
            sl = slice(hh * DK, (hh + 1) * DK)
            q = q_ref[0, hh, rows, :].astype(_f32)
            k = k_ref[0, hh, rows, :].astype(_f32)
            qd = q * eb[:, sl]
            kd = k * enb[:, sl]
            kt = kd * etot[:, sl]
            if d == 0:
                qs = jnp.concatenate([qd[:CHUNK], qd[CHUNK:] * e0[:, sl]], axis=0)
                ks = jnp.concatenate([kt[:CHUNK] * e1[:, sl], kt[CHUNK:]], axis=0)
            else:
                qs = jnp.concatenate([qd[:CHUNK] * e1[:, sl], qd[CHUNK:]], axis=0)
                ks = jnp.concatenate([kt[:CHUNK], kt[CHUNK:] * e0[:, sl]], axis=0)
            qd_ref[slot, hh, d] = qd.astype(_bf16)
            kk_ref[slot, hh, d, 0:R, :] = kd.astype(_bf16)
            kk_ref[slot, hh, d, R:2 * R, :] = kt.astype(_bf16)
            qs_ref[slot, hh, d] = qs.astype(_bf16)
            ks_ref[slot, hh, d] = ks.astype(_bf16)
            gdec_ref[slot, hh, d] = jnp.broadcast_to((e0 * e1)[:, sl], (8, DK))
            v2 = jnp.concatenate([v_ref[0, 2 * hh, rows, :], v_ref[0, 2 * hh + 1, rows, :]], axis=-1)
            vt_ref[slot, hh, d] = v2.T

    gn = gn_ref[0]

    def finish(hh, rows, o):
        o = o + oacc_ref[hh, rows, :]
        ms = jnp.mean(o * o, axis=-1, keepdims=True)
        y = o * lax.rsqrt(ms + EPS) * gn
        z = jnp.concatenate([z_ref[0, 2 * hh, rows, :], z_ref[0, 2 * hh + 1, rows, :]], axis=-1).astype(_f32)
        y = (y * z).astype(_bf16)
        o_ref[0, 2 * hh, rows, :] = y[:, :LANE]
        o_ref[0, 2 * hh + 1, rows, :] = y[:, LANE:]

    def recurrence(d, hh, trips, second_visit):
        firsts = []
        for t, slot in trips:
            rows = grp_rows(t if d == 0 else ngroups - 1 - t)
            v2 = jnp.concatenate([v_ref[0, 2 * hh, rows, :], v_ref[0, 2 * hh + 1, rows, :]], axis=-1)
            p = _nt(qd_ref[slot, hh, d], kk_ref[slot, hh, d])
            up = _nn(vt_ref[slot, hh, d], ks_ref[slot, hh, d])
            firsts.append((rows, v2, p, up, qs_ref[slot, hh, d], gdec_ref[slot, hh, d][0:1]))
        yield
        st = st_ref[hh, d]
        outs = []
        for rows, v2, p, up, qs, gdec in firsts:
            a = jnp.where(mask_d[d], p[:, :R], jnp.where(mask_o[d], p[:, R:], 0.0)).astype(_bf16)
            outs.append((rows, _nn(a, v2) + _nt(qs, st.astype(_bf16))))
            st = st * gdec + up
        st_ref[hh, d] = st
        yield
        for rows, o in outs:
            if second_visit:
                finish(hh, rows, o)
            else:
                oacc_ref[hh, rows, :] = o

    def run_staged(tasks):
        tasks = list(tasks)
        while tasks:
            alive = []
            for task in tasks:
                try:
                    next(task)
                    alive.append(task)
                except StopIteration:
                    pass
            tasks = alive

    def per_trip(fn, pair_idx):
        tasks = []
        for slot in range(GLA_NT):
            t = GLA_NT * pair_idx + slot
            if t < ngroups:
                tasks += [fn(0, t, slot), fn(1, ngroups - 1 - t, slot)]
        return tasks

    st_ref[...] = jnp.zeros_like(st_ref)
    run_staged(per_trip(prefix, 0))
    run_staged(per_trip(operands, 0))
    run_staged(per_trip(prefix, 1))

    for tt in range(ngroups // GLA_NT):
        second_visit = tt >= ngroups // (2 * GLA_NT)
        trips = [(GLA_NT * tt + s, s) for s in range(GLA_NT)]
        rec = [recurrence(d, hh, trips, second_visit) for d in range(2) for hh in range(GLA_NH)]
        run_staged(per_trip(prefix, tt + 2) + rec + per_trip(operands, tt + 1))


def _gla(proj, gates, gla_norm, layer):
    B, _, L, _ = proj.shape
    ngroups = L // GROUP_ROWS
    assert ngroups % (2 * GLA_NT) == 0 and HB % GLA_NH == 0
    nh = GLA_NH
    grid = (B, HB // nh)
    slots = (GLA_NT, nh, 2)
    return pl.pallas_call(
        functools.partial(_gla_kernel, ngroups=ngroups),
        grid=grid,
        in_specs=[
            pl.BlockSpec((1, nh, L, LANE), lambda b, h: (b, CB_QB // nh + h, 0, 0)),
            pl.BlockSpec((1, nh, L, LANE), lambda b, h: (b, CB_KB // nh + h, 0, 0)),
            pl.BlockSpec((1, 2 * nh, L, LANE), lambda b, h: (b, CB_VB // (2 * nh) + h, 0, 0)),
            pl.BlockSpec((1, 2 * nh, L, LANE), lambda b, h: (b, CB_ZB // (2 * nh) + h, 0, 0)),
            pl.BlockSpec((1, nh, L, LANE), lambda b, h: (b, h, 0, 0)),
            pl.BlockSpec((1, nh, L, LANE), lambda b, h: (b, HB // nh + h, 0, 0)),
            pl.BlockSpec((1, 1, DV), lambda b, h: (layer, 0, 0)),
        ],
        out_specs=pl.BlockSpec((1, 2 * nh, L, LANE), lambda b, h: (b, h, 0, 0)),
        out_shape=jax.ShapeDtypeStruct((B, 2 * HB, L, LANE), _bf16),
        scratch_shapes=[
            pltpu.VMEM(slots + (GROUP_ROWS, DK), _bf16),
            pltpu.VMEM(slots + (2 * GROUP_ROWS, DK), _bf16),
            pltpu.VMEM(slots + (GROUP_ROWS, DK), _bf16),
            pltpu.VMEM(slots + (GROUP_ROWS, DK), _bf16),
            pltpu.VMEM(slots + (8, DK), _f32),
            pltpu.VMEM(slots + (DV, GROUP_ROWS), _bf16),
            pltpu.VMEM((GLA_NT, 2, GROUP_ROWS, nh * DK), _f32),
            pltpu.VMEM((nh, 2, DV, DK), _f32),
            pltpu.VMEM((nh, L, DV), _f32),
        ],
        compiler_params=pltpu.CompilerParams(
            dimension_semantics=("parallel", "parallel"),
            vmem_limit_bytes=VMEM_LIMIT),
        name="gla",
    )(proj, proj, proj, proj, gates, gates, gla_norm)


OUT_TM = 512


def _out_proj_kernel(a_ref, g_ref, w_ref, x_ref, nw_ref, o_ref):
    mix = jnp.concatenate([a_ref[0, c] for c in range(HA)] + [g_ref[0, c] for c in range(2 * HB)], axis=-1)
    y = _nn(mix, w_ref[0])
    ms = jnp.mean(y * y, axis=-1, keepdims=True)
    o_ref[0] = x_ref[0] + y * lax.rsqrt(ms + EPS) * nw_ref[0]


def _out_proj(attn, gla, w_out, x, norm_w, layer):
    B, L, _ = x.shape
    grid = (B, L // OUT_TM)
    return pl.pallas_call(
        _out_proj_kernel,
        grid=grid,
        in_specs=[
            pl.BlockSpec((1, HA, OUT_TM, LANE), lambda b, i: (b, 0, i, 0)),
            pl.BlockSpec((1, 2 * HB, OUT_TM, LANE), lambda b, i: (b, 0, i, 0)),
            pl.BlockSpec((1, D_MODEL, D_MODEL), lambda b, i: (layer, 0, 0), pipeline_mode=pl.Buffered(1)),
            pl.BlockSpec((1, OUT_TM, D_MODEL), lambda b, i: (b, i, 0)),
            pl.BlockSpec((1, 1, D_MODEL), lambda b, i: (layer, 0, 0)),
        ],
        out_specs=pl.BlockSpec((1, OUT_TM, D_MODEL), lambda b, i: (b, i, 0)),
        out_shape=jax.ShapeDtypeStruct((B, L, D_MODEL), _f32),
        compiler_params=pltpu.CompilerParams(
            dimension_semantics=("parallel", "parallel"),
            vmem_limit_bytes=VMEM_LIMIT),
        name="out_proj",
    )(attn, gla, w_out, x, norm_w)


def _band_tables():
    qi = np.arange(BLOCK)[:, None]
    kj = np.arange(3 * BLOCK)[None, :]
    rel = kj - BLOCK - qi
    nb = N_BUCKETS // 2
    max_exact = nb // 2
    n = np.abs(rel)
    large = max_exact + (np.log(np.maximum(n, 1) / max_exact) / np.log(MAX_DIST / max_exact)
                         * (nb - max_exact)).astype(np.int32)
    large = np.minimum(large, nb - 1)
    bucket = (rel > 0).astype(np.int32) * nb + np.where(n < max_exact, n, large)
    return bucket.astype(np.int32), (n <= WINDOW)


def _position_bias(rel_bias):
    bucket, in_band = _band_tables()
    onehot = jnp.asarray(np.eye(N_BUCKETS, dtype=np.float32)[bucket])
    pos_bias = jnp.einsum("qkn,nh->hqk", onehot, rel_bias.astype(_f32), precision=lax.Precision.HIGHEST)
    key_block = np.arange(3 * BLOCK)[None, :] // BLOCK
    tables = [jnp.where((in_band & keep)[None], pos_bias * LOG2E, NEG)
              for keep in (key_block >= 0, key_block >= 1, key_block <= 1)]
    return jnp.stack(tables, axis=0)


def _gate_weights(w_gk_fwd, w_gk_bwd):
    zeros = jnp.zeros_like(w_gk_fwd)
    return jnp.concatenate([jnp.concatenate([w_gk_fwd, zeros], axis=1),
                            jnp.concatenate([zeros, w_gk_bwd], axis=1)], axis=2).astype(_bf16)


def _layer(x, layer, pos_bias, w_in, wg, bg, sink, gla_norm, w_out, norm_pre, norm_post):
    proj, gates = _in_proj(x, norm_pre, w_in, wg, bg, layer)
    attn = _attention(proj, pos_bias, sink[layer])
    gla = _gla(proj, gates, gla_norm, layer)
    return _out_proj(attn, gla, w_out, x, norm_post, layer)


def _prepare(rel_bias, w_in, w_gk_fwd, b_gk_fwd, w_gk_bwd, b_gk_bwd, sink, gla_norm, w_out, norm_pre, norm_post):
    wg = _gate_weights(w_gk_fwd, w_gk_bwd)
    bg = jnp.concatenate([b_gk_fwd, b_gk_bwd], axis=-1)[:, None, :]
    col_scale = np.ones((w_in.shape[2],), np.float32)
    col_scale[CB_QA * LANE:CB_QA * LANE + WA] = DH ** -0.5 * LOG2E
    col_scale[CB_QB * LANE:CB_QB * LANE + WBK] = DK ** -0.5
    w_in = jnp.swapaxes(w_in * col_scale, 1, 2).astype(_bf16)
    return (_position_bias(rel_bias), w_in, wg, bg,
            sink, gla_norm[:, None, :], w_out.astype(_bf16), norm_pre[:, None, :], norm_post[:, None, :])


def kernel(x_prompt, x_sample, rel_bias, w_in, w_gk_fwd, b_gk_fwd, w_gk_bwd, b_gk_bwd, sink, gla_norm, w_out, norm_pre, norm_post):
    params = _prepare(rel_bias, w_in, w_gk_fwd, b_gk_fwd, w_gk_bwd, b_gk_bwd, sink, gla_norm, w_out,
                      norm_pre, norm_post)
    y_prompt, y_sample = x_prompt, x_sample
    for l in range(w_in.shape[0]):
        y_prompt = _layer(y_prompt, l, *params)
        y_sample = _layer(y_sample, l, *params)
    return (y_prompt, y_sample)
```

```python
import functools

import jax
import jax.numpy as jnp
import numpy as np
from jax import lax
from jax.experimental import pallas as pl
from jax.experimental.pallas import tpu as pltpu

D_MODEL = 2048
HA, HKV, DH = 8, 2, 128
GROUP = HA // HKV
WINDOW, BLOCK = 128, 128
N_BUCKETS, MAX_DIST = 32, 128
HB, DK, DV = 4, 128, 256
GATE_RANK, GATE_NORM = 16, 16.0
CHUNK = 64
NEG = -1e30
EPS = 1e-6

LANE = 128
GROUP_ROWS = 2 * CHUNK

WA = HA * DH
WKV = HKV * DH
WBK = HB * DK
WB = HB * DV
N_MAIN = 2 * WA + 2 * WKV + 2 * WBK + 2 * WB
CB_QA = 0
CB_KA = CB_QA + WA // LANE
CB_VA = CB_KA + WKV // LANE
CB_ZA = CB_VA + WKV // LANE
CB_QB = CB_ZA + WA // LANE
CB_KB = CB_QB + WBK // LANE
CB_VB = CB_KB + WBK // LANE
CB_ZB = CB_VB + WB // LANE
N_CB = N_MAIN // LANE
N_GATE_CB = 2 * WBK // LANE
GATE_TILES = tuple(range(CB_ZA, CB_QB, 4)) + tuple(range(CB_ZB, N_CB, 4))

VMEM_LIMIT = 56 * 1024 * 1024

LOG2E = 1.4426950408889634

_f32 = jnp.float32
_bf16 = jnp.bfloat16


def _nt(a, b):
    return lax.dot_general(a, b, (((1,), (1,)), ((), ())), preferred_element_type=_f32)


def _nn(a, b):
    return jnp.dot(a, b, preferred_element_type=_f32)


def _silu(z):
    h = 0.5 * z
    return h + h * jnp.tanh(h)


def _log2_sigmoid_scaled(x, s):
    return jnp.minimum(x, 0.0) * (s * LOG2E) - jnp.log2(1.0 + jnp.exp2(jnp.abs(x) * (-LOG2E))) * s


IN_TH = 256
IN_TN = 512
IN_RC = 128


def _in_proj_kernel(x0_ref, xa_ref, xb_ref, nw_ref, w_ref, wg_ref, bg_ref, o_ref, g_ref, xn_ref):
    g = pl.program_id(0)
    nw = nw_ref[0]

    def normalise(x_ref, slot):
        for r in range(IN_TH // IN_RC):
            rows = slice(r * IN_RC, (r + 1) * IN_RC)
            x = x_ref[0, rows, :]
            ms = jnp.mean(x * x, axis=-1, keepdims=True)
            xn_ref[slot, rows, :] = (x * lax.rsqrt(ms + EPS) * nw).astype(_bf16)

    def project(slot):
        rows = slice(slot * IN_TH, (slot + 1) * IN_TH)
        xp = xn_ref[slot]
        lr = _nt(xp, w_ref[0, N_MAIN:N_MAIN + 2 * GATE_RANK, :]).astype(_bf16)
        tiles = sorted(range(N_MAIN // IN_TN), key=lambda j: j * IN_TN // LANE not in GATE_TILES)
        for n, j in enumerate(tiles):
            acc = _nt(xp, w_ref[0, j * IN_TN:(j + 1) * IN_TN, :])
            if j * IN_TN // LANE in GATE_TILES:
                acc = _silu(acc)
            for c in range(IN_TN // LANE):
                o_ref[0, j * (IN_TN // LANE) + c, rows, :] = acc[:, c * LANE:(c + 1) * LANE].astype(_bf16)
            if n == 0:
                gates = _log2_sigmoid_scaled(_nn(lr, wg_ref[0]) + bg_ref[0], 1.0 / GATE_NORM).astype(_bf16)
                for c in range(N_GATE_CB):
                    g_ref[0, c, rows, :] = gates[:, c * LANE:(c + 1) * LANE]

    @pl.when(g == 0)
    def _():
        normalise(x0_ref, 0)

    normalise(xa_ref, 1)
    project(0)
    normalise(xb_ref, 0)
    project(1)


def _in_proj(x, norm_w, w_in, wg, bg, layer):
    B, L, _ = x.shape
    nh = L // IN_TH
    halves = B * nh
    steps = halves // 2

    def half_map(offset):
        def index(g):
            t = jnp.minimum(2 * g + offset, halves - 1)
            return (t // nh, t % nh, 0)
        return index

    def out_map(g):
        return (g // (nh // 2), 0, g % (nh // 2), 0)

    half_spec = (1, IN_TH, D_MODEL)
    return pl.pallas_call(
        _in_proj_kernel,
        grid=(steps,),
        in_specs=[
            pl.BlockSpec(half_spec, lambda g: (0, 0, 0)),
            pl.BlockSpec(half_spec, half_map(1)),
            pl.BlockSpec(half_spec, half_map(2)),
            pl.BlockSpec((1, 1, D_MODEL), lambda g: (layer, 0, 0)),
            pl.BlockSpec((1, w_in.shape[1], D_MODEL), lambda g: (layer, 0, 0), pipeline_mode=pl.Buffered(1)),
            pl.BlockSpec((1, 2 * GATE_RANK, 2 * WBK), lambda g: (layer, 0, 0)),
            pl.BlockSpec((1, 1, 2 * WBK), lambda g: (layer, 0, 0)),
        ],
        out_specs=[
            pl.BlockSpec((1, N_CB, 2 * IN_TH, LANE), out_map),
            pl.BlockSpec((1, N_GATE_CB, 2 * IN_TH, LANE), out_map),
        ],
        out_shape=[
            jax.ShapeDtypeStruct((B, N_CB, L, LANE), _bf16),
            jax.ShapeDtypeStruct((B, N_GATE_CB, L, LANE), _bf16),
        ],
        scratch_shapes=[pltpu.VMEM((2, IN_TH, D_MODEL), _bf16)],
        compiler_params=pltpu.CompilerParams(
            dimension_semantics=("arbitrary",),
            vmem_limit_bytes=VMEM_LIMIT),
        name="in_proj",
    )(x, x, x, norm_w, w_in, wg, bg)


def _attn_kernel(sink_ref, q_ref, k_ref, v_ref, z_ref, bias_ref, o_ref, s_ref, p_ref, t_ref, *, nblk):
    h = pl.program_id(1)

    def blk_rows(j):
        return slice(j * BLOCK, (j + 1) * BLOCK)

    def window(ref, blk):
        lb = max(blk - 1, 0)
        rb = min(blk + 1, nblk - 1)
        return jnp.concatenate([ref[0, 0, blk_rows(lb), :], ref[0, 0, blk_rows(blk), :],
                                ref[0, 0, blk_rows(rb), :]], axis=0)

    def stage_a(blk, slot):
        q4 = q_ref[0, :, blk_rows(blk), :].reshape(GROUP * BLOCK, DH)
        s_ref[slot] = _nt(q4, window(k_ref, blk))

    def stage_b(blk, src, dst):
        table = 1 if blk == 0 else (2 if blk == nblk - 1 else 0)
        for g in range(GROUP):
            gr = slice(g * BLOCK, (g + 1) * BLOCK)
            sg = s_ref[src, gr, :] + bias_ref[table, g]
            sk = sink_ref[h * GROUP + g] * LOG2E
            m = jnp.maximum(jnp.max(sg, axis=-1, keepdims=True), sk)
            p_ref[dst, gr, :] = jnp.exp2(sg - m).astype(_bf16)
            t_ref[dst, g] = jnp.broadcast_to(jnp.exp2(sk - m), (BLOCK, DH))

    def stage_c(blk, src):
        vw = window(v_ref, blk)
        vw = jnp.concatenate([vw, jnp.ones_like(vw)], axis=-1)
        o4 = _nn(p_ref[src], vw)
        rows = blk_rows(blk)
        for g in range(GROUP):
            og = o4[g * BLOCK:(g + 1) * BLOCK]
            z = z_ref[0, g, rows, :].astype(_f32)
            den = og[:, DH:] + t_ref[src, g]
            o_ref[0, g, rows, :] = (og[:, :DH] * (1.0 / den) * z).astype(_bf16)

    for it in range(nblk + 2):
        par = it % 2
        if 0 <= it - 2 < nblk:
            stage_c(it - 2, 1 - par)
        if 0 <= it - 1 < nblk:
            stage_b(it - 1, 1 - par, par)
        if it < nblk:
            stage_a(it, par)


def _attention(proj, pos_bias, sink):
    B, _, L, _ = proj.shape
    nblk = L // BLOCK
    assert nblk >= 2 and nblk % 2 == 0
    grid = (B, HKV)
    return pl.pallas_call(
        functools.partial(_attn_kernel, nblk=nblk),
        grid=grid,
        in_specs=[
            pl.BlockSpec(memory_space=pltpu.SMEM),
            pl.BlockSpec((1, GROUP, L, LANE), lambda b, h: (b, CB_QA // GROUP + h, 0, 0)),
            pl.BlockSpec((1, 1, L, LANE), lambda b, h: (b, CB_KA + h, 0, 0)),
            pl.BlockSpec((1, 1, L, LANE), lambda b, h: (b, CB_VA + h, 0, 0)),
            pl.BlockSpec((1, GROUP, L, LANE), lambda b, h: (b, CB_ZA // GROUP + h, 0, 0)),
            pl.BlockSpec((3, GROUP, BLOCK, 3 * BLOCK), lambda b, h: (0, h, 0, 0)),
        ],
        out_specs=pl.BlockSpec((1, GROUP, L, LANE), lambda b, h: (b, h, 0, 0)),
        out_shape=jax.ShapeDtypeStruct((B, HA, L, LANE), _bf16),
        scratch_shapes=[
            pltpu.VMEM((2, GROUP * BLOCK, 3 * BLOCK), _f32),
            pltpu.VMEM((2, GROUP * BLOCK, 3 * BLOCK), _bf16),
            pltpu.VMEM((2, GROUP, BLOCK, DH), _f32),
        ],
        compiler_params=pltpu.CompilerParams(
            dimension_semantics=("parallel", "parallel"),
            vmem_limit_bytes=VMEM_LIMIT),
        name="attention",
    )(sink, proj, proj, proj, proj, pos_bias)


GLA_NH = 1
GLA_NT = 2


def _gla_kernel(q_ref, k_ref, v_ref, z_ref, gf_ref, gb_ref, gn_ref,
                o_ref,
                qd_ref, kk_ref, qs_ref, ks_ref, gdec_ref, vt_ref, pre_ref, st_ref, oacc_ref, *, ngroups):
    R = GROUP_ROWS
    row = lax.broadcasted_iota(jnp.int32, (R, R), 0)
    colm = lax.broadcasted_iota(jnp.int32, (R, R), 1)
    same_chunk = (row >= CHUNK) == (colm >= CHUNK)
    tri = jnp.where(same_chunk & (colm <= row), 1.0, 0.0).astype(_bf16)
    mask_d = (same_chunk & (colm <= row), same_chunk & (colm >= row))
    mask_o = ((row >= CHUNK) & (colm < CHUNK), (row < CHUNK) & (colm >= CHUNK))
    g_refs = (gf_ref, gb_ref)

    def grp_rows(g):
        return slice(g * R, (g + 1) * R)

    def log_gates(d, rows):
        return jnp.concatenate([g_refs[d][0, hh, rows, :] for hh in range(GLA_NH)], axis=-1)

    def rep(x):
        return jnp.broadcast_to(x, (CHUNK, x.shape[-1]))

    def prefix(d, g, slot):
        pre = _nn(tri, log_gates(d, grp_rows(g)))
        yield
        yield
        pre_ref[slot, d] = pre

    def operands(d, g, slot):
        yield
        rows = grp_rows(g)
        pre = pre_ref[slot, d]
        t0, t1 = pre[CHUNK - 1:CHUNK], pre[R - 1:R]
        e0, e1 = jnp.exp2(t0), jnp.exp2(t1)
        tot = jnp.concatenate([rep(t0), rep(t1)], axis=0)
        b = pre if d == 0 else tot - pre + log_gates(d, rows).astype(_f32)
        eb, enb = jnp.exp2(b), jnp.exp2(-b)
        etot = jnp.concatenate([rep(e0), rep(e1)], axis=0)
        for hh in range(GLA_NH):
            sl = slice(hh * DK, (hh + 1) * DK)
            q = q_ref[0, hh, rows, :].astype(_f32)
            k = k_ref[0, hh, rows, :].astype(_f32)
            qd = q * eb[:, sl]
            kd = k * enb[:, sl]
            kt = kd * etot[:, sl]
            if d == 0:
                qs = jnp.concatenate([qd[:CHUNK], qd[CHUNK:] * e0[:, sl]], axis=0)
                ks = jnp.concatenate([kt[:CHUNK] * e1[:, sl], kt[CHUNK:]], axis=0)
            else:
                qs = jnp.concatenate([qd[:CHUNK] * e1[:, sl], qd[CHUNK:]], axis=0)
                ks = jnp.concatenate([kt[:CHUNK], kt[CHUNK:] * e0[:, sl]], axis=0)
            qd_ref[slot, hh, d] = qd.astype(_bf16)
            kk_ref[slot, hh, d, 0:R, :] = kd.astype(_bf16)
            kk_ref[slot, hh, d, R:2 * R, :] = kt.astype(_bf16)
            qs_ref[slot, hh, d] = qs.astype(_bf16)
            ks_ref[slot, hh, d] = ks.astype(_bf16)
            gdec_ref[slot, hh, d] = jnp.broadcast_to((e0 * e1)[:, sl], (8, DK))
            v2 = jnp.concatenate([v_ref[0, 2 * hh, rows, :], v_ref[0, 2 * hh + 1, rows, :]], axis=-1)
            vt_ref[slot, hh, d] = v2.T

    gn = gn_ref[0]

    def finish(hh, rows, o):
        o = o + oacc_ref[hh, rows, :]
        ms = jnp.mean(o * o, axis=-1, keepdims=True)
        y = o * lax.rsqrt(ms + EPS) * gn
        z = jnp.concatenate([z_ref[0, 2 * hh, rows, :], z_ref[0, 2 * hh + 1, rows, :]], axis=-1).astype(_f32)
        y = (y * z).astype(_bf16)
        o_ref[0, 2 * hh, rows, :] = y[:, :LANE]
        o_ref[0, 2 * hh + 1, rows, :] = y[:, LANE:]

    def recurrence(d, hh, trips, second_visit):
        firsts = []
        for t, slot in trips:
            rows = grp_rows(t if d == 0 else ngroups - 1 - t)
            v2 = jnp.concatenate([v_ref[0, 2 * hh, rows, :], v_ref[0, 2 * hh + 1, rows, :]], axis=-1)
            p = _nt(qd_ref[slot, hh, d], kk_ref[slot, hh, d])
            up = _nn(vt_ref[slot, hh, d], ks_ref[slot, hh, d])
            firsts.append((rows, v2, p, up, qs_ref[slot, hh, d], gdec_ref[slot, hh, d][0:1]))
        yield
        st = st_ref[hh, d]
        outs = []
        for rows, v2, p, up, qs, gdec in firsts:
            a = jnp.where(mask_d[d], p[:, :R], jnp.where(mask_o[d], p[:, R:], 0.0)).astype(_bf16)
            outs.append((rows, _nn(a, v2) + _nt(qs, st.astype(_bf16))))
            st = st * gdec + up
        st_ref[hh, d] = st
        yield
        for rows, o in outs:
            if second_visit:
                finish(hh, rows, o)
            else:
                oacc_ref[hh, rows, :] = o

    def run_staged(tasks):
        tasks = list(tasks)
        while tasks:
            alive = []
            for task in tasks:
                try:
                    next(task)
                    alive.append(task)
                except StopIteration:
                    pass
            tasks = alive

    def per_trip(fn, pair_idx):
        tasks = []
        for slot in range(GLA_NT):
            t = GLA_NT * pair_idx + slot
            if t < ngroups:
                tasks += [fn(0, t, slot), fn(1, ngroups - 1 - t, slot)]
        return tasks

    st_ref[...] = jnp.zeros_like(st_ref)
    run_staged(per_trip(prefix, 0))
    run_staged(per_trip(operands, 0))
    run_staged(per_trip(prefix, 1))

    for tt in range(ngroups // GLA_NT):
        second_visit = tt >= ngroups // (2 * GLA_NT)
        trips = [(GLA_NT * tt + s, s) for s in range(GLA_NT)]
        rec = [recurrence(d, hh, trips, second_visit) for d in range(2) for hh in range(GLA_NH)]
        run_staged(per_trip(prefix, tt + 2) + rec + per_trip(operands, tt + 1))


def _gla(proj, gates, gla_norm, layer):
    B, _, L, _ = proj.shape
    ngroups = L // GROUP_ROWS
    assert ngroups % (2 * GLA_NT) == 0 and HB % GLA_NH == 0
    nh = GLA_NH
    grid = (B, HB // nh)
    slots = (GLA_NT, nh, 2)
    return pl.pallas_call(
        functools.partial(_gla_kernel, ngroups=ngroups),
        grid=grid,
        in_specs=[
            pl.BlockSpec((1, nh, L, LANE), lambda b, h: (b, CB_QB // nh + h, 0, 0)),
            pl.BlockSpec((1, nh, L, LANE), lambda b, h: (b, CB_KB // nh + h, 0, 0)),
            pl.BlockSpec((1, 2 * nh, L, LANE), lambda b, h: (b, CB_VB // (2 * nh) + h, 0, 0)),
            pl.BlockSpec((1, 2 * nh, L, LANE), lambda b, h: (b, CB_ZB // (2 * nh) + h, 0, 0)),
            pl.BlockSpec((1, nh, L, LANE), lambda b, h: (b, h, 0, 0)),
            pl.BlockSpec((1, nh, L, LANE), lambda b, h: (b, HB // nh + h, 0, 0)),
            pl.BlockSpec((1, 1, DV), lambda b, h: (layer, 0, 0)),
        ],
        out_specs=pl.BlockSpec((1, 2 * nh, L, LANE), lambda b, h: (b, h, 0, 0)),
        out_shape=jax.ShapeDtypeStruct((B, 2 * HB, L, LANE), _bf16),
        scratch_shapes=[
            pltpu.VMEM(slots + (GROUP_ROWS, DK), _bf16),
            pltpu.VMEM(slots + (2 * GROUP_ROWS, DK), _bf16),
            pltpu.VMEM(slots + (GROUP_ROWS, DK), _bf16),
            pltpu.VMEM(slots + (GROUP_ROWS, DK), _bf16),
            pltpu.VMEM(slots + (8, DK), _f32),
            pltpu.VMEM(slots + (DV, GROUP_ROWS), _bf16),
            pltpu.VMEM((GLA_NT, 2, GROUP_ROWS, nh * DK), _f32),
            pltpu.VMEM((nh, 2, DV, DK), _f32),
            pltpu.VMEM((nh, L, DV), _f32),
        ],
        compiler_params=pltpu.CompilerParams(
            dimension_semantics=("parallel", "parallel"),
            vmem_limit_bytes=VMEM_LIMIT),
        name="gla",
    )(proj, proj, proj, proj, gates, gates, gla_norm)


OUT_TM = 1024


def _out_proj_kernel(a_ref, g_ref, w_ref, x_ref, nw_ref, o_ref):
    mix = jnp.concatenate([a_ref[0, c] for c in range(HA)] + [g_ref[0, c] for c in range(2 * HB)], axis=-1)
    y = _nn(mix, w_ref[0])
    ms = jnp.mean(y * y, axis=-1, keepdims=True)
    o_ref[0] = x_ref[0] + y * lax.rsqrt(ms + EPS) * nw_ref[0]


def _out_proj(attn, gla, w_out, x, norm_w, layer):
    B, L, _ = x.shape
    grid = (B, L // OUT_TM)
    return pl.pallas_call(
        _out_proj_kernel,
        grid=grid,
        in_specs=[
            pl.BlockSpec((1, HA, OUT_TM, LANE), lambda b, i: (b, 0, i, 0)),
            pl.BlockSpec((1, 2 * HB, OUT_TM, LANE), lambda b, i: (b, 0, i, 0)),
            pl.BlockSpec((1, D_MODEL, D_MODEL), lambda b, i: (layer, 0, 0), pipeline_mode=pl.Buffered(1)),
            pl.BlockSpec((1, OUT_TM, D_MODEL), lambda b, i: (b, i, 0)),
            pl.BlockSpec((1, 1, D_MODEL), lambda b, i: (layer, 0, 0)),
        ],
        out_specs=pl.BlockSpec((1, OUT_TM, D_MODEL), lambda b, i: (b, i, 0)),
        out_shape=jax.ShapeDtypeStruct((B, L, D_MODEL), _f32),
        compiler_params=pltpu.CompilerParams(
            dimension_semantics=("parallel", "parallel"),
            vmem_limit_bytes=VMEM_LIMIT),
        name="out_proj",
    )(attn, gla, w_out, x, norm_w)


def _band_tables():
    qi = np.arange(BLOCK)[:, None]
    kj = np.arange(3 * BLOCK)[None, :]
    rel = kj - BLOCK - qi
    nb = N_BUCKETS // 2
    max_exact = nb // 2
    n = np.abs(rel)
    large = max_exact + (np.log(np.maximum(n, 1) / max_exact) / np.log(MAX_DIST / max_exact)
                         * (nb - max_exact)).astype(np.int32)
    large = np.minimum(large, nb - 1)
    bucket = (rel > 0).astype(np.int32) * nb + np.where(n < max_exact, n, large)
    return bucket.astype(np.int32), (n <= WINDOW)


def _position_bias(rel_bias):
    bucket, in_band = _band_tables()
    onehot = jnp.asarray(np.eye(N_BUCKETS, dtype=np.float32)[bucket])
    pos_bias = jnp.einsum("qkn,nh->hqk", onehot, rel_bias.astype(_f32), precision=lax.Precision.HIGHEST)
    key_block = np.arange(3 * BLOCK)[None, :] // BLOCK
    tables = [jnp.where((in_band & keep)[None], pos_bias * LOG2E, NEG)
              for keep in (key_block >= 0, key_block >= 1, key_block <= 1)]
    return jnp.stack(tables, axis=0)


def _gate_weights(w_gk_fwd, w_gk_bwd):
    zeros = jnp.zeros_like(w_gk_fwd)
    return jnp.concatenate([jnp.concatenate([w_gk_fwd, zeros], axis=1),
                            jnp.concatenate([zeros, w_gk_bwd], axis=1)], axis=2).astype(_bf16)


def _layer(x, layer, pos_bias, w_in, wg, bg, sink, gla_norm, w_out, norm_pre, norm_post):
    proj, gates = _in_proj(x, norm_pre, w_in, wg, bg, layer)
    attn = _attention(proj, pos_bias, sink[layer])
    gla = _gla(proj, gates, gla_norm, layer)
    return _out_proj(attn, gla, w_out, x, norm_post, layer)


def _prepare(rel_bias, w_in, w_gk_fwd, b_gk_fwd, w_gk_bwd, b_gk_bwd, sink, gla_norm, w_out, norm_pre, norm_post):
    wg = _gate_weights(w_gk_fwd, w_gk_bwd)
    bg = jnp.concatenate([b_gk_fwd, b_gk_bwd], axis=-1)[:, None, :]
    col_scale = np.ones((w_in.shape[2],), np.float32)
    col_scale[CB_QA * LANE:CB_QA * LANE + WA] = DH ** -0.5 * LOG2E
    col_scale[CB_QB * LANE:CB_QB * LANE + WBK] = DK ** -0.5
    w_in = jnp.swapaxes(w_in * col_scale, 1, 2).astype(_bf16)
    return (_position_bias(rel_bias), w_in, wg, bg,
            sink, gla_norm[:, None, :], w_out.astype(_bf16), norm_pre[:, None, :], norm_post[:, None, :])


def kernel(x_prompt, x_sample, rel_bias, w_in, w_gk_fwd, b_gk_fwd, w_gk_bwd, b_gk_bwd, sink, gla_norm, w_out, norm_pre, norm_post):
    params = _prepare(rel_bias, w_in, w_gk_fwd, b_gk_fwd, w_gk_bwd, b_gk_bwd, sink, gla_norm, w_out,
                      norm_pre, norm_post)
    y_prompt, y_sample = x_prompt, x_sample
    for l in range(w_in.shape[0]):
        y_prompt = _layer(y_prompt, l, *params)
        y_sample = _layer(y_sample, l, *params)
    return (y_prompt, y_sample)
```

```python
import functools

import jax
import jax.numpy as jnp
import numpy as np
from jax import lax
from jax.experimental import pallas as pl
from jax.experimental.pallas import tpu as pltpu

D_MODEL = 2048
HA, HKV, DH = 8, 2, 128
GROUP = HA // HKV
WINDOW, BLOCK = 128, 128
N_BUCKETS, MAX_DIST = 32, 128
HB, DK, DV = 4, 128, 256
GATE_RANK, GATE_NORM = 16, 16.0
CHUNK = 64
NEG = -1e30
EPS = 1e-6

LANE = 128
GROUP_ROWS = 2 * CHUNK

WA = HA * DH
WKV = HKV * DH
WBK = HB * DK
WB = HB * DV
N_MAIN = 2 * WA + 2 * WKV + 2 * WBK + 2 * WB
CB_QA = 0
CB_KA = CB_QA + WA // LANE
CB_VA = CB_KA + WKV // LANE
CB_ZA = CB_VA + WKV // LANE
CB_QB = CB_ZA + WA // LANE
CB_KB = CB_QB + WBK // LANE
CB_VB = CB_KB + WBK // LANE
CB_ZB = CB_VB + WB // LANE
N_CB = N_MAIN // LANE
N_GATE_CB = 2 * WBK // LANE


def _is_gate_block(cb):
    return CB_ZA <= cb < CB_QB or CB_ZB <= cb < N_CB

VMEM_LIMIT = 56 * 1024 * 1024

LOG2E = 1.4426950408889634

_f32 = jnp.float32
_bf16 = jnp.bfloat16


def _nt(a, b):
    return lax.dot_general(a, b, (((1,), (1,)), ((), ())), preferred_element_type=_f32)


def _nn(a, b):
    return jnp.dot(a, b, preferred_element_type=_f32)


def _silu(z):
    h = 0.5 * z
    return h + h * jnp.tanh(h)


def _log2_sigmoid_scaled(x, s):
    return jnp.minimum(x, 0.0) * (s * LOG2E) - jnp.log2(1.0 + jnp.exp2(jnp.abs(x) * (-LOG2E))) * s


IN_TH = 256
IN_RC = 128


def _in_proj_kernel(x0_ref, xa_ref, xb_ref, nw_ref, w_ref, wg_ref, bg_ref, o_ref, g_ref, xn_ref):
    g = pl.program_id(0)
    nw = nw_ref[0]

    def normalise(x_ref, slot):
        for r in range(IN_TH // IN_RC):
            rows = slice(r * IN_RC, (r + 1) * IN_RC)
            x = x_ref[0, rows, :]
            ms = jnp.mean(x * x, axis=-1, keepdims=True)
            xn_ref[slot, rows, :] = (x * lax.rsqrt(ms + EPS) * nw).astype(_bf16)

    def project(slot):
        rows = slice(slot * IN_TH, (slot + 1) * IN_TH)
        xp = xn_ref[slot]
        lr = _nt(xp, w_ref[0, N_MAIN:N_MAIN + 2 * GATE_RANK, :]).astype(_bf16)
        tiles = [(CB_ZA, CB_QB), (CB_ZB, N_CB), (CB_QA, CB_KA), (CB_QB, CB_VB), (CB_VB, CB_ZB), (CB_KA, CB_ZA)]
        for n, (cb0, cb1) in enumerate(tiles):
            acc = _nt(xp, w_ref[0, cb0 * LANE:cb1 * LANE, :])
            if _is_gate_block(cb0):
                acc = _silu(acc)
            for c in range(cb1 - cb0):
                o_ref[0, cb0 + c, rows, :] = acc[:, c * LANE:(c + 1) * LANE].astype(_bf16)
            if n == 0:
                gates = _log2_sigmoid_scaled(_nn(lr, wg_ref[0]) + bg_ref[0], 1.0 / GATE_NORM).astype(_bf16)
                for c in range(N_GATE_CB):
                    g_ref[0, c, rows, :] = gates[:, c * LANE:(c + 1) * LANE]

    @pl.when(g == 0)
    def _():
        normalise(x0_ref, 0)

    normalise(xa_ref, 1)
    project(0)
    normalise(xb_ref, 0)
    project(1)


def _in_proj(x, norm_w, w_in, wg, bg, layer):
    B, L, _ = x.shape
    nh = L // IN_TH
    halves = B * nh
    steps = halves // 2

    def half_map(offset):
        def index(g):
            t = jnp.minimum(2 * g + offset, halves - 1)
            return (t // nh, t % nh, 0)
        return index

    def out_map(g):
        return (g // (nh // 2), 0, g % (nh // 2), 0)

    half_spec = (1, IN_TH, D_MODEL)
    return pl.pallas_call(
        _in_proj_kernel,
        grid=(steps,),
        in_specs=[
            pl.BlockSpec(half_spec, lambda g: (0, 0, 0)),
            pl.BlockSpec(half_spec, half_map(1)),
            pl.BlockSpec(half_spec, half_map(2)),
            pl.BlockSpec((1, 1, D_MODEL), lambda g: (layer, 0, 0)),
            pl.BlockSpec((1, w_in.shape[1], D_MODEL), lambda g: (layer, 0, 0), pipeline_mode=pl.Buffered(1)),
            pl.BlockSpec((1, 2 * GATE_RANK, 2 * WBK), lambda g: (layer, 0, 0)),
            pl.BlockSpec((1, 1, 2 * WBK), lambda g: (layer, 0, 0)),
        ],
        out_specs=[
            pl.BlockSpec((1, N_CB, 2 * IN_TH, LANE), out_map),
            pl.BlockSpec((1, N_GATE_CB, 2 * IN_TH, LANE), out_map),
        ],
        out_shape=[
            jax.ShapeDtypeStruct((B, N_CB, L, LANE), _bf16),
            jax.ShapeDtypeStruct((B, N_GATE_CB, L, LANE), _bf16),
        ],
        scratch_shapes=[pltpu.VMEM((2, IN_TH, D_MODEL), _bf16)],
        compiler_params=pltpu.CompilerParams(
            dimension_semantics=("arbitrary",),
            vmem_limit_bytes=VMEM_LIMIT),
        name="in_proj",
    )(x, x, x, norm_w, w_in, wg, bg)


def _attn_kernel(sink_ref, q_ref, k_ref, v_ref, z_ref, bias_ref, o_ref, s_ref, p_ref, t_ref, *, nblk):
    h = pl.program_id(1)

    def blk_rows(j):
        return slice(j * BLOCK, (j + 1) * BLOCK)

    def window(ref, blk):
        lb = max(blk - 1, 0)
        rb = min(blk + 1, nblk - 1)
        return jnp.concatenate([ref[0, 0, blk_rows(lb), :], ref[0, 0, blk_rows(blk), :],
                                ref[0, 0, blk_rows(rb), :]], axis=0)

    def stage_a(blk, slot):
        q4 = q_ref[0, :, blk_rows(blk), :].reshape(GROUP * BLOCK, DH)
        s_ref[slot] = _nt(q4, window(k_ref, blk))

    def stage_b(blk, src, dst):
        table = 1 if blk == 0 else (2 if blk == nblk - 1 else 0)
        for g in range(GROUP):
            gr = slice(g * BLOCK, (g + 1) * BLOCK)
            sg = s_ref[src, gr, :] + bias_ref[table, g]
            sk = sink_ref[h * GROUP + g] * LOG2E
            m = jnp.maximum(jnp.max(sg, axis=-1, keepdims=True), sk)
            p_ref[dst, gr, :] = jnp.exp2(sg - m).astype(_bf16)
            t_ref[dst, g] = jnp.broadcast_to(jnp.exp2(sk - m), (BLOCK, DH))

    def stage_c(blk, src):
        vw = window(v_ref, blk)
        vw = jnp.concatenate([vw, jnp.ones_like(vw)], axis=-1)
        o4 = _nn(p_ref[src], vw)
        rows = blk_rows(blk)
        for g in range(GROUP):
            og = o4[g * BLOCK:(g + 1) * BLOCK]
            z = z_ref[0, g, rows, :].astype(_f32)
            den = og[:, DH:] + t_ref[src, g]
            o_ref[0, g, rows, :] = (og[:, :DH] * (1.0 / den) * z).astype(_bf16)

    for it in range(nblk + 2):
        par = it % 2
        if 0 <= it - 2 < nblk:
            stage_c(it - 2, 1 - par)
        if 0 <= it - 1 < nblk:
            stage_b(it - 1, 1 - par, par)
        if it < nblk:
            stage_a(it, par)


def _attention(proj, pos_bias, sink):
    B, _, L, _ = proj.shape
    nblk = L // BLOCK
    assert nblk >= 2 and nblk % 2 == 0
    grid = (B, HKV)
    return pl.pallas_call(
        functools.partial(_attn_kernel, nblk=nblk),
        grid=grid,
        in_specs=[
            pl.BlockSpec(memory_space=pltpu.SMEM),
            pl.BlockSpec((1, GROUP, L, LANE), lambda b, h: (b, CB_QA // GROUP + h, 0, 0)),
            pl.BlockSpec((1, 1, L, LANE), lambda b, h: (b, CB_KA + h, 0, 0)),
            pl.BlockSpec((1, 1, L, LANE), lambda b, h: (b, CB_VA + h, 0, 0)),
            pl.BlockSpec((1, GROUP, L, LANE), lambda b, h: (b, CB_ZA // GROUP + h, 0, 0)),
            pl.BlockSpec((3, GROUP, BLOCK, 3 * BLOCK), lambda b, h: (0, h, 0, 0)),
        ],
        out_specs=pl.BlockSpec((1, GROUP, L, LANE), lambda b, h: (b, h, 0, 0)),
        out_shape=jax.ShapeDtypeStruct((B, HA, L, LANE), _bf16),
        scratch_shapes=[
            pltpu.VMEM((2, GROUP * BLOCK, 3 * BLOCK), _f32),
            pltpu.VMEM((2, GROUP * BLOCK, 3 * BLOCK), _bf16),
            pltpu.VMEM((2, GROUP, BLOCK, DH), _f32),
        ],
        compiler_params=pltpu.CompilerParams(
            dimension_semantics=("parallel", "parallel"),
            vmem_limit_bytes=VMEM_LIMIT),
        name="attention",
    )(sink, proj, proj, proj, proj, pos_bias)


GLA_NH = 1
GLA_NT = 2


def _gla_kernel(q_ref, k_ref, v_ref, z_ref, gf_ref, gb_ref, gn_ref,
                o_ref,
                qd_ref, kk_ref, qs_ref, ks_ref, gdec_ref, vt_ref, pre_ref, st_ref, oacc_ref, *, ngroups):
    R = GROUP_ROWS
    row = lax.broadcasted_iota(jnp.int32, (R, R), 0)
    colm = lax.broadcasted_iota(jnp.int32, (R, R), 1)
    same_chunk = (row >= CHUNK) == (colm >= CHUNK)
    tri = jnp.where(same_chunk & (colm <= row), 1.0, 0.0).astype(_bf16)
    mask_d = (same_chunk & (colm <= row), same_chunk & (colm >= row))
    mask_o = ((row >= CHUNK) & (colm < CHUNK), (row < CHUNK) & (colm >= CHUNK))
    g_refs = (gf_ref, gb_ref)

    def grp_rows(g):
        return slice(g * R, (g + 1) * R)

    def log_gates(d, rows):
        return jnp.concatenate([g_refs[d][0, hh, rows, :] for hh in range(GLA_NH)], axis=-1)

    def rep(x):
        return jnp.broadcast_to(x, (CHUNK, x.shape[-1]))

    def prefix(d, g, slot):
        pre = _nn(tri, log_gates(d, grp_rows(g)))
        yield
        yield
        pre_ref[slot, d] = pre

    def operands(d, g, slot):
        yield
        rows = grp_rows(g)
        pre = pre_ref[slot, d]
        t0, t1 = pre[CHUNK - 1:CHUNK], pre[R - 1:R]
        e0, e1 = jnp.exp2(t0), jnp.exp2(t1)
        tot = jnp.concatenate([rep(t0), rep(t1)], axis=0)
        b = pre if d == 0 else tot - pre + log_gates(d, rows).astype(_f32)
        eb, enb = jnp.exp2(b), jnp.exp2(-b)
        etot = jnp.concatenate([rep(e0), rep(e1)], axis=0)
        for hh in range(GLA_NH):
            sl = slice(hh * DK, (hh + 1) * DK)
            q = q_ref[0, hh, rows, :].astype(_f32)
            k = k_ref[0, hh, rows, :].astype(_f32)
            qd = q * eb[:, sl]
            kd = k * enb[:, sl]
            kt = kd * etot[:, sl]
            if d == 0:
                qs = jnp.concatenate([qd[:CHUNK], qd[CHUNK:] * e0[:, sl]], axis=0)
                ks = jnp.concatenate([kt[:CHUNK] * e1[:, sl], kt[CHUNK:]], axis=0)
            else:
                qs = jnp.concatenate([qd[:CHUNK] * e1[:, sl], qd[CHUNK:]], axis=0)
                ks = jnp.concatenate([kt[:CHUNK], kt[CHUNK:] * e0[:, sl]], axis=0)
            qd_ref[slot, hh, d] = qd.astype(_bf16)
            kk_ref[slot, hh, d, 0:R, :] = kd.astype(_bf16)
            kk_ref[slot, hh, d, R:2 * R, :] = kt.astype(_bf16)
            qs_ref[slot, hh, d] = qs.astype(_bf16)
            ks_ref[slot, hh, d] = ks.astype(_bf16)
            gdec_ref[slot, hh, d] = jnp.broadcast_to((e0 * e1)[:, sl], (8, DK))
            v2 = jnp.concatenate([v_ref[0, 2 * hh, rows, :], v_ref[0, 2 * hh + 1, rows, :]], axis=-1)
            vt_ref[slot, hh, d] = v2.T

    gn = gn_ref[0]

    def finish(hh, rows, o):
        o = o + oacc_ref[hh, rows, :]
        ms = jnp.mean(o * o, axis=-1, keepdims=True)
        y = o * lax.rsqrt(ms + EPS) * gn
        z = jnp.concatenate([z_ref[0, 2 * hh, rows, :], z_ref[0, 2 * hh + 1, rows, :]], axis=-1).astype(_f32)
        y = (y * z).astype(_bf16)
        o_ref[0, 2 * hh, rows, :] = y[:, :LANE]
        o_ref[0, 2 * hh + 1, rows, :] = y[:, LANE:]

    def recurrence(d, hh, trips, second_visit):
        firsts = []
        for t, slot in trips:
            rows = grp_rows(t if d == 0 else ngroups - 1 - t)
            v2 = jnp.concatenate([v_ref[0, 2 * hh, rows, :], v_ref[0, 2 * hh + 1, rows, :]], axis=-1)
            p = _nt(qd_ref[slot, hh, d], kk_ref[slot, hh, d])
            up = _nn(vt_ref[slot, hh, d], ks_ref[slot, hh, d])
            firsts.append((rows, v2, p, up, qs_ref[slot, hh, d], gdec_ref[slot, hh, d][0:1]))
        yield
        st = st_ref[hh, d]
        outs = []
        for rows, v2, p, up, qs, gdec in firsts:
            a = jnp.where(mask_d[d], p[:, :R], jnp.where(mask_o[d], p[:, R:], 0.0)).astype(_bf16)
            outs.append((rows, _nn(a, v2) + _nt(qs, st.astype(_bf16))))
            st = st * gdec + up
        st_ref[hh, d] = st
        yield
        for rows, o in outs:
            if second_visit:
                finish(hh, rows, o)
            else:
                oacc_ref[hh, rows, :] = o

    def run_staged(tasks):
        tasks = list(tasks)
        while tasks:
            alive = []
            for task in tasks:
                try:
                    next(task)
                    alive.append(task)
                except StopIteration:
                    pass
            tasks = alive

    def per_trip(fn, pair_idx):
        tasks = []
        for slot in range(GLA_NT):
            t = GLA_NT * pair_idx + slot
            if t < ngroups:
                tasks += [fn(0, t, slot), fn(1, ngroups - 1 - t, slot)]
        return tasks

    st_ref[...] = jnp.zeros_like(st_ref)
    run_staged(per_trip(prefix, 0))
    run_staged(per_trip(operands, 0))
    run_staged(per_trip(prefix, 1))

    for tt in range(ngroups // GLA_NT):
        second_visit = tt >= ngroups // (2 * GLA_NT)
        trips = [(GLA_NT * tt + s, s) for s in range(GLA_NT)]
        rec = [recurrence(d, hh, trips, second_visit) for d in range(2) for hh in range(GLA_NH)]
        run_staged(per_trip(prefix, tt + 2) + rec + per_trip(operands, tt + 1))


def _gla(proj, gates, gla_norm, layer):
    B, _, L, _ = proj.shape
    ngroups = L // GROUP_ROWS
    assert ngroups % (2 * GLA_NT) == 0 and HB % GLA_NH == 0
    nh = GLA_NH
    grid = (B, HB // nh)
    slots = (GLA_NT, nh, 2)
    return pl.pallas_call(
        functools.partial(_gla_kernel, ngroups=ngroups),
        grid=grid,
        in_specs=[
            pl.BlockSpec((1, nh, L, LANE), lambda b, h: (b, CB_QB // nh + h, 0, 0)),
            pl.BlockSpec((1, nh, L, LANE), lambda b, h: (b, CB_KB // nh + h, 0, 0)),
            pl.BlockSpec((1, 2 * nh, L, LANE), lambda b, h: (b, CB_VB // (2 * nh) + h, 0, 0)),
            pl.BlockSpec((1, 2 * nh, L, LANE), lambda b, h: (b, CB_ZB // (2 * nh) + h, 0, 0)),
            pl.BlockSpec((1, nh, L, LANE), lambda b, h: (b, h, 0, 0)),
            pl.BlockSpec((1, nh, L, LANE), lambda b, h: (b, HB // nh + h, 0, 0)),
            pl.BlockSpec((1, 1, DV), lambda b, h: (layer, 0, 0)),
        ],
        out_specs=pl.BlockSpec((1, 2 * nh, L, LANE), lambda b, h: (b, h, 0, 0)),
        out_shape=jax.ShapeDtypeStruct((B, 2 * HB, L, LANE), _bf16),
        scratch_shapes=[
            pltpu.VMEM(slots + (GROUP_ROWS, DK), _bf16),
            pltpu.VMEM(slots + (2 * GROUP_ROWS, DK), _bf16),
            pltpu.VMEM(slots + (GROUP_ROWS, DK), _bf16),
            pltpu.VMEM(slots + (GROUP_ROWS, DK), _bf16),
            pltpu.VMEM(slots + (8, DK), _f32),
            pltpu.VMEM(slots + (DV, GROUP_ROWS), _bf16),
            pltpu.VMEM((GLA_NT, 2, GROUP_ROWS, nh * DK), _f32),
            pltpu.VMEM((nh, 2, DV, DK), _f32),
            pltpu.VMEM((nh, L, DV), _f32),
        ],
        compiler_params=pltpu.CompilerParams(
            dimension_semantics=("parallel", "parallel"),
            vmem_limit_bytes=VMEM_LIMIT),
        name="gla",
    )(proj, proj, proj, proj, gates, gates, gla_norm)


OUT_TM = 1024


OUT_TH = 512


def _out_proj_kernel(a_ref, g_ref, w_ref, x_ref, nw_ref, o_ref):
    nw = nw_ref[0]
    for r in range(OUT_TM // OUT_TH):
        rows = slice(r * OUT_TH, (r + 1) * OUT_TH)
        mix = jnp.concatenate([a_ref[0, c, rows, :] for c in range(HA)]
                              + [g_ref[0, c, rows, :] for c in range(2 * HB)], axis=-1)
        y = _nn(mix, w_ref[0])
        ms = jnp.mean(y * y, axis=-1, keepdims=True)
        o_ref[0, rows, :] = x_ref[0, rows, :] + y * lax.rsqrt(ms + EPS) * nw


def _out_proj(attn, gla, w_out, x, norm_w, layer):
    B, L, _ = x.shape
    grid = (B, L // OUT_TM)
    return pl.pallas_call(
        _out_proj_kernel,
        grid=grid,
        in_specs=[
            pl.BlockSpec((1, HA, OUT_TM, LANE), lambda b, i: (b, 0, i, 0)),
            pl.BlockSpec((1, 2 * HB, OUT_TM, LANE), lambda b, i: (b, 0, i, 0)),
            pl.BlockSpec((1, D_MODEL, D_MODEL), lambda b, i: (layer, 0, 0), pipeline_mode=pl.Buffered(1)),
            pl.BlockSpec((1, OUT_TM, D_MODEL), lambda b, i: (b, i, 0)),
            pl.BlockSpec((1, 1, D_MODEL), lambda b, i: (layer, 0, 0)),
        ],
        out_specs=pl.BlockSpec((1, OUT_TM, D_MODEL), lambda b, i: (b, i, 0)),
        out_shape=jax.ShapeDtypeStruct((B, L, D_MODEL), _f32),
        compiler_params=pltpu.CompilerParams(
            dimension_semantics=("parallel", "parallel"),
            vmem_limit_bytes=VMEM_LIMIT),
        name="out_proj",
    )(attn, gla, w_out, x, norm_w)


def _band_tables():
    qi = np.arange(BLOCK)[:, None]
    kj = np.arange(3 * BLOCK)[None, :]
    rel = kj - BLOCK - qi
    nb = N_BUCKETS // 2
    max_exact = nb // 2
    n = np.abs(rel)
    large = max_exact + (np.log(np.maximum(n, 1) / max_exact) / np.log(MAX_DIST / max_exact)
                         * (nb - max_exact)).astype(np.int32)
    large = np.minimum(large, nb - 1)
    bucket = (rel > 0).astype(np.int32) * nb + np.where(n < max_exact, n, large)
    return bucket.astype(np.int32), (n <= WINDOW)


def _position_bias(rel_bias):
    bucket, in_band = _band_tables()
    onehot = jnp.asarray(np.eye(N_BUCKETS, dtype=np.float32)[bucket])
    pos_bias = jnp.einsum("qkn,nh->hqk", onehot, rel_bias.astype(_f32), precision=lax.Precision.HIGHEST)
    key_block = np.arange(3 * BLOCK)[None, :] // BLOCK
    tables = [jnp.where((in_band & keep)[None], pos_bias * LOG2E, NEG)
              for keep in (key_block >= 0, key_block >= 1, key_block <= 1)]
    return jnp.stack(tables, axis=0)


def _gate_weights(w_gk_fwd, w_gk_bwd):
    zeros = jnp.zeros_like(w_gk_fwd)
    return jnp.concatenate([jnp.concatenate([w_gk_fwd, zeros], axis=1),
                            jnp.concatenate([zeros, w_gk_bwd], axis=1)], axis=2).astype(_bf16)


def _layer(x, layer, pos_bias, w_in, wg, bg, sink, gla_norm, w_out, norm_pre, norm_post):
    proj, gates = _in_proj(x, norm_pre, w_in, wg, bg, layer)
    attn = _attention(proj, pos_bias, sink[layer])
    gla = _gla(proj, gates, gla_norm, layer)
    return _out_proj(attn, gla, w_out, x, norm_post, layer)


def _prepare(rel_bias, w_in, w_gk_fwd, b_gk_fwd, w_gk_bwd, b_gk_bwd, sink, gla_norm, w_out, norm_pre, norm_post):
    wg = _gate_weights(w_gk_fwd, w_gk_bwd)
    bg = jnp.concatenate([b_gk_fwd, b_gk_bwd], axis=-1)[:, None, :]
    col_scale = np.ones((w_in.shape[2],), np.float32)
    col_scale[CB_QA * LANE:CB_QA * LANE + WA] = DH ** -0.5 * LOG2E
    col_scale[CB_QB * LANE:CB_QB * LANE + WBK] = DK ** -0.5
    w_in = jnp.swapaxes(w_in * col_scale, 1, 2).astype(_bf16)
    return (_position_bias(rel_bias), w_in, wg, bg,
            sink, gla_norm[:, None, :], w_out.astype(_bf16), norm_pre[:, None, :], norm_post[:, None, :])


def kernel(x_prompt, x_sample, rel_bias, w_in, w_gk_fwd, b_gk_fwd, w_gk_bwd, b_gk_bwd, sink, gla_norm, w_out, norm_pre, norm_post):
    params = _prepare(rel_bias, w_in, w_gk_fwd, b_gk_fwd, w_gk_bwd, b_gk_bwd, sink, gla_norm, w_out,
                      norm_pre, norm_post)
    y_prompt, y_sample = x_prompt, x_sample
    for l in range(w_in.shape[0]):
        y_prompt = _layer(y_prompt, l, *params)
        y_sample = _layer(y_sample, l, *params)
    return (y_prompt, y_sample)
```

```python
import functools

import jax
import jax.numpy as jnp
import numpy as np
from jax import lax
from jax.experimental import pallas as pl
from jax.experimental.pallas import tpu as pltpu

D_MODEL = 2048
HA, HKV, DH = 8, 2, 128
GROUP = HA // HKV
WINDOW, BLOCK = 128, 128
N_BUCKETS, MAX_DIST = 32, 128
HB, DK, DV = 4, 128, 256
GATE_RANK, GATE_NORM = 16, 16.0
CHUNK = 64
NEG = -1e30
EPS = 1e-6

LANE = 128
GROUP_ROWS = 2 * CHUNK

WA = HA * DH
WKV = HKV * DH
WBK = HB * DK
WB = HB * DV
N_MAIN = 2 * WA + 2 * WKV + 2 * WBK + 2 * WB
CB_QA = 0
CB_KA = CB_QA + WA // LANE
CB_VA = CB_KA + WKV // LANE
CB_ZA = CB_VA + WKV // LANE
CB_QB = CB_ZA + WA // LANE
CB_KB = CB_QB + WBK // LANE
CB_VB = CB_KB + WBK // LANE
CB_ZB = CB_VB + WB // LANE
N_CB = N_MAIN // LANE
N_GATE_CB = 2 * WBK // LANE


def _is_gate_block(cb):
    return CB_ZA <= cb < CB_QB or CB_ZB <= cb < N_CB

VMEM_LIMIT = 58 * 1024 * 1024

LOG2E = 1.4426950408889634

_f32 = jnp.float32
_bf16 = jnp.bfloat16


def _nt(a, b):
    return lax.dot_general(a, b, (((1,), (1,)), ((), ())), preferred_element_type=_f32)


def _nn(a, b):
    return jnp.dot(a, b, preferred_element_type=_f32)


def _silu(z):
    h = 0.5 * z
    return h + h * jnp.tanh(h)


def _log2_sigmoid_scaled(x, s):
    return jnp.minimum(x, 0.0) * (s * LOG2E) - jnp.log2(1.0 + jnp.exp2(jnp.abs(x) * (-LOG2E))) * s


IN_TH = 256
IN_RC = 128


def _in_proj_kernel(x0_ref, xap_ref, xas_ref, xbp_ref, xbs_ref, nw_ref, w_ref, wg_ref, bg_ref, o_ref, g_ref, xn_ref,
                    *, prompt_halves):
    g = pl.program_id(0)
    nw = nw_ref[0]

    def normalise(x_ref, slot, xs_ref=None, half=None):
        for r in range(IN_TH // IN_RC):
            rows = slice(r * IN_RC, (r + 1) * IN_RC)
            x = x_ref[0, rows, :]
            if xs_ref is not None:
                x = jnp.where(half < prompt_halves, x, xs_ref[0, rows, :])
            ms = jnp.mean(x * x, axis=-1, keepdims=True)
            xn_ref[slot, rows, :] = (x * lax.rsqrt(ms + EPS) * nw).astype(_bf16)

    def project(slot):
        rows = slice(slot * IN_TH, (slot + 1) * IN_TH)
        xp = xn_ref[slot]
        lr = _nt(xp, w_ref[0, N_MAIN:N_MAIN + 2 * GATE_RANK, :]).astype(_bf16)
        tiles = [(CB_ZA, CB_QB), (CB_ZB, N_CB), (CB_QA, CB_KA), (CB_QB, CB_VB), (CB_VB, CB_ZB), (CB_KA, CB_ZA)]
        for n, (cb0, cb1) in enumerate(tiles):
            acc = _nt(xp, w_ref[0, cb0 * LANE:cb1 * LANE, :])
            if _is_gate_block(cb0):
                acc = _silu(acc)
            for c in range(cb1 - cb0):
                o_ref[0, cb0 + c, rows, :] = acc[:, c * LANE:(c + 1) * LANE].astype(_bf16)
            if n == 0:
                gates = _log2_sigmoid_scaled(_nn(lr, wg_ref[0]) + bg_ref[0], 1.0 / GATE_NORM).astype(_bf16)
                for c in range(N_GATE_CB):
                    g_ref[0, c, rows, :] = gates[:, c * LANE:(c + 1) * LANE]

    @pl.when(g == 0)
    def _():
        normalise(x0_ref, 0)

    normalise(xap_ref, 1, xas_ref, 2 * g + 1)
    project(0)
    normalise(xbp_ref, 0, xbs_ref, 2 * g + 2)
    project(1)


def _in_proj(x_prompt, x_sample, norm_w, w_in, wg, bg, layer):
    bp, L, _ = x_prompt.shape
    bs = x_sample.shape[0]
    B = bp + bs
    nh = L // IN_TH
    hp, hs = bp * nh, bs * nh
    steps = (hp + hs) // 2

    def half_map(offset, first, count):
        def index(g):
            t = jnp.clip(2 * g + offset - first, 0, count - 1)
            return (t // nh, t % nh, 0)
        return index

    def out_map(g):
        return (g // (nh // 2), 0, g % (nh // 2), 0)

    half_spec = (1, IN_TH, D_MODEL)
    return pl.pallas_call(
        functools.partial(_in_proj_kernel, prompt_halves=hp),
        grid=(steps,),
        in_specs=[
            pl.BlockSpec(half_spec, lambda g: (0, 0, 0), pipeline_mode=pl.Buffered(1)),
            pl.BlockSpec(half_spec, half_map(1, 0, hp)),
            pl.BlockSpec(half_spec, half_map(1, hp, hs)),
            pl.BlockSpec(half_spec, half_map(2, 0, hp)),
            pl.BlockSpec(half_spec, half_map(2, hp, hs)),
            pl.BlockSpec((1, 1, D_MODEL), lambda g: (layer, 0, 0)),
            pl.BlockSpec((1, w_in.shape[1], D_MODEL), lambda g: (layer, 0, 0), pipeline_mode=pl.Buffered(1)),
            pl.BlockSpec((1, 2 * GATE_RANK, 2 * WBK), lambda g: (layer, 0, 0)),
            pl.BlockSpec((1, 1, 2 * WBK), lambda g: (layer, 0, 0)),
        ],
        out_specs=[
            pl.BlockSpec((1, N_CB, 2 * IN_TH, LANE), out_map),
            pl.BlockSpec((1, N_GATE_CB, 2 * IN_TH, LANE), out_map),
        ],
        out_shape=[
            jax.ShapeDtypeStruct((B, N_CB, L, LANE), _bf16),
            jax.ShapeDtypeStruct((B, N_GATE_CB, L, LANE), _bf16),
        ],
        scratch_shapes=[pltpu.VMEM((2, IN_TH, D_MODEL), _bf16)],
        compiler_params=pltpu.CompilerParams(
            dimension_semantics=("arbitrary",),
            vmem_limit_bytes=VMEM_LIMIT),
        name="in_proj",
    )(x_prompt, x_prompt, x_sample, x_prompt, x_sample, norm_w, w_in, wg, bg)


def _attn_kernel(sink_ref, q_ref, k_ref, v_ref, z_ref, bias_ref, o_ref, s_ref, p_ref, t_ref, *, nblk):
    h = pl.program_id(1)

    def blk_rows(j):
        return slice(j * BLOCK, (j + 1) * BLOCK)

    def window(ref, blk):
        lb = max(blk - 1, 0)
        rb = min(blk + 1, nblk - 1)
        return jnp.concatenate([ref[0, 0, blk_rows(lb), :], ref[0, 0, blk_rows(blk), :],
                                ref[0, 0, blk_rows(rb), :]], axis=0)

    def stage_a(blk, slot):
        q4 = q_ref[0, :, blk_rows(blk), :].reshape(GROUP * BLOCK, DH)
        s_ref[slot] = _nt(q4, window(k_ref, blk))

    def stage_b(blk, src, dst):
        table = 1 if blk == 0 else (2 if blk == nblk - 1 else 0)
        for g in range(GROUP):
            gr = slice(g * BLOCK, (g + 1) * BLOCK)
            sg = s_ref[src, gr, :] + bias_ref[table, g]
            sk = sink_ref[h * GROUP + g] * LOG2E
            m = jnp.maximum(jnp.max(sg, axis=-1, keepdims=True), sk)
            p_ref[dst, gr, :] = jnp.exp2(sg - m).astype(_bf16)
            t_ref[dst, g] = jnp.broadcast_to(jnp.exp2(sk - m), (BLOCK, DH))

    def stage_c(blk, src):
        vw = window(v_ref, blk)
        vw = jnp.concatenate([vw, jnp.ones_like(vw)], axis=-1)
        o4 = _nn(p_ref[src], vw)
        rows = blk_rows(blk)
        for g in range(GROUP):
            og = o4[g * BLOCK:(g + 1) * BLOCK]
            z = z_ref[0, g, rows, :].astype(_f32)
            den = og[:, DH:] + t_ref[src, g]
            o_ref[0, g, rows, :] = (og[:, :DH] * (1.0 / den) * z).astype(_bf16)

    for it in range(nblk + 2):
        par = it % 2
        if 0 <= it - 2 < nblk:
            stage_c(it - 2, 1 - par)
        if 0 <= it - 1 < nblk:
            stage_b(it - 1, 1 - par, par)
        if it < nblk:
            stage_a(it, par)


def _attention(proj, pos_bias, sink):
    B, _, L, _ = proj.shape
    nblk = L // BLOCK
    assert nblk >= 2 and nblk % 2 == 0
    grid = (B, HKV)
    return pl.pallas_call(
        functools.partial(_attn_kernel, nblk=nblk),
        grid=grid,
        in_specs=[
            pl.BlockSpec(memory_space=pltpu.SMEM),
            pl.BlockSpec((1, GROUP, L, LANE), lambda b, h: (b, CB_QA // GROUP + h, 0, 0)),
            pl.BlockSpec((1, 1, L, LANE), lambda b, h: (b, CB_KA + h, 0, 0)),
            pl.BlockSpec((1, 1, L, LANE), lambda b, h: (b, CB_VA + h, 0, 0)),
            pl.BlockSpec((1, GROUP, L, LANE), lambda b, h: (b, CB_ZA // GROUP + h, 0, 0)),
            pl.BlockSpec((3, GROUP, BLOCK, 3 * BLOCK), lambda b, h: (0, h, 0, 0)),
        ],
        out_specs=pl.BlockSpec((1, GROUP, L, LANE), lambda b, h: (b, h, 0, 0)),
        out_shape=jax.ShapeDtypeStruct((B, HA, L, LANE), _bf16),
        scratch_shapes=[
            pltpu.VMEM((2, GROUP * BLOCK, 3 * BLOCK), _f32),
            pltpu.VMEM((2, GROUP * BLOCK, 3 * BLOCK), _bf16),
            pltpu.VMEM((2, GROUP, BLOCK, DH), _f32),
        ],
        compiler_params=pltpu.CompilerParams(
            dimension_semantics=("parallel", "parallel"),
            vmem_limit_bytes=VMEM_LIMIT),
        name="attention",
    )(sink, proj, proj, proj, proj, pos_bias)


GLA_NH = 1
GLA_NT = 2


def _gla_kernel(q_ref, k_ref, v_ref, z_ref, gf_ref, gb_ref, gn_ref,
                o_ref,
                qd_ref, kk_ref, qs_ref, ks_ref, gdec_ref, vt_ref, pre_ref, st_ref, oacc_ref, *, ngroups):
    R = GROUP_ROWS
    row = lax.broadcasted_iota(jnp.int32, (R, R), 0)
    colm = lax.broadcasted_iota(jnp.int32, (R, R), 1)
    same_chunk = (row >= CHUNK) == (colm >= CHUNK)
    tri = jnp.where(same_chunk & (colm <= row), 1.0, 0.0).astype(_bf16)
    mask_d = (same_chunk & (colm <= row), same_chunk & (colm >= row))
    mask_o = ((row >= CHUNK) & (colm < CHUNK), (row < CHUNK) & (colm >= CHUNK))
    g_refs = (gf_ref, gb_ref)

    def grp_rows(g):
        return slice(g * R, (g + 1) * R)

    def log_gates(d, rows):
        return jnp.concatenate([g_refs[d][0, hh, rows, :] for hh in range(GLA_NH)], axis=-1)

    def rep(x):
        return jnp.broadcast_to(x, (CHUNK, x.shape[-1]))

    def prefix(d, g, slot):
        pre = _nn(tri, log_gates(d, grp_rows(g)))
        yield
        yield
        pre_ref[slot, d] = pre

    def operands(d, g, slot):
        yield
        rows = grp_rows(g)
        pre = pre_ref[slot, d]
        t0, t1 = pre[CHUNK - 1:CHUNK], pre[R - 1:R]
        e0, e1 = jnp.exp2(t0), jnp.exp2(t1)
        tot = jnp.concatenate([rep(t0), rep(t1)], axis=0)
        b = pre if d == 0 else tot - pre + log_gates(d, rows).astype(_f32)
        eb, enb = jnp.exp2(b), jnp.exp2(-b)
        etot = jnp.concatenate([rep(e0), rep(e1)], axis=0)
        for hh in range(GLA_NH):
            sl = slice(hh * DK, (hh + 1) * DK)
            q = q_ref[0, hh, rows, :].astype(_f32)
            k = k_ref[0, hh, rows, :].astype(_f32)
            qd = q * eb[:, sl]
            kd = k * enb[:, sl]
            kt = kd * etot[:, sl]
            if d == 0:
                qs = jnp.concatenate([qd[:CHUNK], qd[CHUNK:] * e0[:, sl]], axis=0)
                ks = jnp.concatenate([kt[:CHUNK] * e1[:, sl], kt[CHUNK:]], axis=0)
            else:
                qs = jnp.concatenate([qd[:CHUNK] * e1[:, sl], qd[CHUNK:]], axis=0)
                ks = jnp.concatenate([kt[:CHUNK], kt[CHUNK:] * e0[:, sl]], axis=0)
            qd_ref[slot, hh, d] = qd.astype(_bf16)
            kk_ref[slot, hh, d, 0:R, :] = kd.astype(_bf16)
            kk_ref[slot, hh, d, R:2 * R, :] = kt.astype(_bf16)
            qs_ref[slot, hh, d] = qs.astype(_bf16)
            ks_ref[slot, hh, d] = ks.astype(_bf16)
            gdec_ref[slot, hh, d] = jnp.broadcast_to((e0 * e1)[:, sl], (8, DK))
            v2 = jnp.concatenate([v_ref[0, 2 * hh, rows, :], v_ref[0, 2 * hh + 1, rows, :]], axis=-1)
            vt_ref[slot, hh, d] = v2.T

    gn = gn_ref[0]

    def finish(hh, rows, o):
        o = o + oacc_ref[hh, rows, :]
        ms = jnp.mean(o * o, axis=-1, keepdims=True)
        y = o * lax.rsqrt(ms + EPS) * gn
        z = jnp.concatenate([z_ref[0, 2 * hh, rows, :], z_ref[0, 2 * hh + 1, rows, :]], axis=-1).astype(_f32)
        y = (y * z).astype(_bf16)
        o_ref[0, 2 * hh, rows, :] = y[:, :LANE]
        o_ref[0, 2 * hh + 1, rows, :] = y[:, LANE:]

    def recurrence(d, hh, trips, second_visit):
        firsts = []
        for t, slot in trips:
            rows = grp_rows(t if d == 0 else ngroups - 1 - t)
            v2 = jnp.concatenate([v_ref[0, 2 * hh, rows, :], v_ref[0, 2 * hh + 1, rows, :]], axis=-1)
            p = _nt(qd_ref[slot, hh, d], kk_ref[slot, hh, d])
            up = _nn(vt_ref[slot, hh, d], ks_ref[slot, hh, d])
            firsts.append((rows, v2, p, up, qs_ref[slot, hh, d], gdec_ref[slot, hh, d][0:1]))
        yield
        st = st_ref[hh, d]
        outs = []
        for rows, v2, p, up, qs, gdec in firsts:
            a = jnp.where(mask_d[d], p[:, :R], jnp.where(mask_o[d], p[:, R:], 0.0)).astype(_bf16)
            outs.append((rows, _nn(a, v2) + _nt(qs, st.astype(_bf16))))
            st = st * gdec + up
        st_ref[hh, d] = st
        yield
        for rows, o in outs:
            if second_visit:
                finish(hh, rows, o)
            else:
                oacc_ref[hh, rows, :] = o

    def run_staged(tasks):
        tasks = list(tasks)
        while tasks:
            alive = []
            for task in tasks:
                try:
                    next(task)
                    alive.append(task)
                except StopIteration:
                    pass
            tasks = alive

    def per_trip(fn, pair_idx):
        tasks = []
        for slot in range(GLA_NT):
            t = GLA_NT * pair_idx + slot
            if t < ngroups:
                tasks += [fn(0, t, slot), fn(1, ngroups - 1 - t, slot)]
        return tasks

    st_ref[...] = jnp.zeros_like(st_ref)
    run_staged(per_trip(prefix, 0))
    run_staged(per_trip(operands, 0))
    run_staged(per_trip(prefix, 1))

    for tt in range(ngroups // GLA_NT):
        second_visit = tt >= ngroups // (2 * GLA_NT)
        trips = [(GLA_NT * tt + s, s) for s in range(GLA_NT)]
        rec = [recurrence(d, hh, trips, second_visit) for d in range(2) for hh in range(GLA_NH)]
        run_staged(per_trip(prefix, tt + 2) + rec + per_trip(operands, tt + 1))


def _gla(proj, gates, gla_norm, layer):
    B, _, L, _ = proj.shape
    ngroups = L // GROUP_ROWS
    assert ngroups % (2 * GLA_NT) == 0 and HB % GLA_NH == 0
    nh = GLA_NH
    grid = (B, HB // nh)
    slots = (GLA_NT, nh, 2)
    return pl.pallas_call(
        functools.partial(_gla_kernel, ngroups=ngroups),
        grid=grid,
        in_specs=[
            pl.BlockSpec((1, nh, L, LANE), lambda b, h: (b, CB_QB // nh + h, 0, 0)),
            pl.BlockSpec((1, nh, L, LANE), lambda b, h: (b, CB_KB // nh + h, 0, 0)),
            pl.BlockSpec((1, 2 * nh, L, LANE), lambda b, h: (b, CB_VB // (2 * nh) + h, 0, 0)),
            pl.BlockSpec((1, 2 * nh, L, LANE), lambda b, h: (b, CB_ZB // (2 * nh) + h, 0, 0)),
            pl.BlockSpec((1, nh, L, LANE), lambda b, h: (b, h, 0, 0)),
            pl.BlockSpec((1, nh, L, LANE), lambda b, h: (b, HB // nh + h, 0, 0)),
            pl.BlockSpec((1, 1, DV), lambda b, h: (layer, 0, 0)),
        ],
        out_specs=pl.BlockSpec((1, 2 * nh, L, LANE), lambda b, h: (b, h, 0, 0)),
        out_shape=jax.ShapeDtypeStruct((B, 2 * HB, L, LANE), _bf16),
        scratch_shapes=[
            pltpu.VMEM(slots + (GROUP_ROWS, DK), _bf16),
            pltpu.VMEM(slots + (2 * GROUP_ROWS, DK), _bf16),
            pltpu.VMEM(slots + (GROUP_ROWS, DK), _bf16),
            pltpu.VMEM(slots + (GROUP_ROWS, DK), _bf16),
            pltpu.VMEM(slots + (8, DK), _f32),
            pltpu.VMEM(slots + (DV, GROUP_ROWS), _bf16),
            pltpu.VMEM((GLA_NT, 2, GROUP_ROWS, nh * DK), _f32),
            pltpu.VMEM((nh, 2, DV, DK), _f32),
            pltpu.VMEM((nh, L, DV), _f32),
        ],
        compiler_params=pltpu.CompilerParams(
            dimension_semantics=("parallel", "parallel"),
            vmem_limit_bytes=VMEM_LIMIT),
        name="gla",
    )(proj, proj, proj, proj, gates, gates, gla_norm)


OUT_TM = 1024


OUT_TH = 512


def _out_proj_kernel(a_ref, g_ref, w_ref, x_ref, nw_ref, o_ref):
    nw = nw_ref[0]
    for r in range(OUT_TM // OUT_TH):
        rows = slice(r * OUT_TH, (r + 1) * OUT_TH)
        mix = jnp.concatenate([a_ref[0, c, rows, :] for c in range(HA)]
                              + [g_ref[0, c, rows, :] for c in range(2 * HB)], axis=-1)
        y = _nn(mix, w_ref[0])
        ms = jnp.mean(y * y, axis=-1, keepdims=True)
        o_ref[0, rows, :] = x_ref[0, rows, :] + y * lax.rsqrt(ms + EPS) * nw


def _out_proj(attn, gla, w_out, x, norm_w, layer, row0):
    B, L, _ = x.shape
    grid = (B, L // OUT_TM)
    return pl.pallas_call(
        _out_proj_kernel,
        grid=grid,
        in_specs=[
            pl.BlockSpec((1, HA, OUT_TM, LANE), lambda b, i: (row0 + b, 0, i, 0)),
            pl.BlockSpec((1, 2 * HB, OUT_TM, LANE), lambda b, i: (row0 + b, 0, i, 0)),
            pl.BlockSpec((1, D_MODEL, D_MODEL), lambda b, i: (layer, 0, 0), pipeline_mode=pl.Buffered(1)),
            pl.BlockSpec((1, OUT_TM, D_MODEL), lambda b, i: (b, i, 0)),
            pl.BlockSpec((1, 1, D_MODEL), lambda b, i: (layer, 0, 0)),
        ],
        out_specs=pl.BlockSpec((1, OUT_TM, D_MODEL), lambda b, i: (b, i, 0)),
        out_shape=jax.ShapeDtypeStruct((B, L, D_MODEL), _f32),
        compiler_params=pltpu.CompilerParams(
            dimension_semantics=("parallel", "parallel"),
            vmem_limit_bytes=VMEM_LIMIT),
        name="out_proj",
    )(attn, gla, w_out, x, norm_w)


def _band_tables():
    qi = np.arange(BLOCK)[:, None]
    kj = np.arange(3 * BLOCK)[None, :]
    rel = kj - BLOCK - qi
    nb = N_BUCKETS // 2
    max_exact = nb // 2
    n = np.abs(rel)
    large = max_exact + (np.log(np.maximum(n, 1) / max_exact) / np.log(MAX_DIST / max_exact)
                         * (nb - max_exact)).astype(np.int32)
    large = np.minimum(large, nb - 1)
    bucket = (rel > 0).astype(np.int32) * nb + np.where(n < max_exact, n, large)
    return bucket.astype(np.int32), (n <= WINDOW)


def _position_bias(rel_bias):
    bucket, in_band = _band_tables()
    onehot = jnp.asarray(np.eye(N_BUCKETS, dtype=np.float32)[bucket])
    pos_bias = jnp.einsum("qkn,nh->hqk", onehot, rel_bias.astype(_f32), precision=lax.Precision.HIGHEST)
    key_block = np.arange(3 * BLOCK)[None, :] // BLOCK
    tables = [jnp.where((in_band & keep)[None], pos_bias * LOG2E, NEG)
              for keep in (key_block >= 0, key_block >= 1, key_block <= 1)]
    return jnp.stack(tables, axis=0)


def _gate_weights(w_gk_fwd, w_gk_bwd):
    zeros = jnp.zeros_like(w_gk_fwd)
    return jnp.concatenate([jnp.concatenate([w_gk_fwd, zeros], axis=1),
                            jnp.concatenate([zeros, w_gk_bwd], axis=1)], axis=2).astype(_bf16)


def _layer(x_prompt, x_sample, layer, pos_bias, w_in, wg, bg, sink, gla_norm, w_out, norm_pre, norm_post):
    proj, gates = _in_proj(x_prompt, x_sample, norm_pre, w_in, wg, bg, layer)
    attn = _attention(proj, pos_bias, sink[layer])
    gla = _gla(proj, gates, gla_norm, layer)
    y_prompt = _out_proj(attn, gla, w_out, x_prompt, norm_post, layer, 0)
    y_sample = _out_proj(attn, gla, w_out, x_sample, norm_post, layer, x_prompt.shape[0])
    return y_prompt, y_sample


def _prepare(rel_bias, w_in, w_gk_fwd, b_gk_fwd, w_gk_bwd, b_gk_bwd, sink, gla_norm, w_out, norm_pre, norm_post):
    wg = _gate_weights(w_gk_fwd, w_gk_bwd)
    bg = jnp.concatenate([b_gk_fwd, b_gk_bwd], axis=-1)[:, None, :]
    col_scale = np.ones((w_in.shape[2],), np.float32)
    col_scale[CB_QA * LANE:CB_QA * LANE + WA] = DH ** -0.5 * LOG2E
    col_scale[CB_QB * LANE:CB_QB * LANE + WBK] = DK ** -0.5
    w_in = jnp.swapaxes(w_in * col_scale, 1, 2).astype(_bf16)
    return (_position_bias(rel_bias), w_in, wg, bg,
            sink, gla_norm[:, None, :], w_out.astype(_bf16), norm_pre[:, None, :], norm_post[:, None, :])


def kernel(x_prompt, x_sample, rel_bias, w_in, w_gk_fwd, b_gk_fwd, w_gk_bwd, b_gk_bwd, sink, gla_norm, w_out, norm_pre, norm_post):
    params = _prepare(rel_bias, w_in, w_gk_fwd, b_gk_fwd, w_gk_bwd, b_gk_bwd, sink, gla_norm, w_out,
                      norm_pre, norm_post)
    y_prompt, y_sample = x_prompt, x_sample
    for l in range(w_in.shape[0]):
        y_prompt, y_sample = _layer(y_prompt, y_sample, l, *params)
    return (y_prompt, y_sample)
```

```python
import functools

import jax
import jax.numpy as jnp
import numpy as np
from jax import lax
from jax.experimental import pallas as pl
from jax.experimental.pallas import tpu as pltpu

D_MODEL = 2048
HA, HKV, DH = 8, 2, 128
GROUP = HA // HKV
WINDOW, BLOCK = 128, 128
N_BUCKETS, MAX_DIST = 32, 128
HB, DK, DV = 4, 128, 256
GATE_RANK, GATE_NORM = 16, 16.0
CHUNK = 64
NEG = -1e30
EPS = 1e-6

LANE = 128
GROUP_ROWS = 2 * CHUNK

WA = HA * DH
WKV = HKV * DH
WBK = HB * DK
WB = HB * DV
N_MAIN = 2 * WA + 2 * WKV + 2 * WBK + 2 * WB
CB_QA = 0
CB_KA = CB_QA + WA // LANE
CB_VA = CB_KA + WKV // LANE
CB_ZA = CB_VA + WKV // LANE
CB_QB = CB_ZA + WA // LANE
CB_KB = CB_QB + WBK // LANE
CB_VB = CB_KB + WBK // LANE
CB_ZB = CB_VB + WB // LANE
N_CB = N_MAIN // LANE
N_GATE_CB = 2 * WBK // LANE


def _is_gate_block(cb):
    return CB_ZA <= cb < CB_QB or CB_ZB <= cb < N_CB

VMEM_LIMIT = 58 * 1024 * 1024

LOG2E = 1.4426950408889634

_f32 = jnp.float32
_bf16 = jnp.bfloat16


def _nt(a, b):
    return lax.dot_general(a, b, (((1,), (1,)), ((), ())), preferred_element_type=_f32)


def _nn(a, b):
    return jnp.dot(a, b, preferred_element_type=_f32)


def _silu(z):
    h = 0.5 * z
    return h + h * jnp.tanh(h)


def _log2_sigmoid_scaled(x, s):
    return jnp.minimum(x, 0.0) * (s * LOG2E) - jnp.log2(1.0 + jnp.exp2(jnp.abs(x) * (-LOG2E))) * s


IN_TH = 256
IN_RC = 128


def _in_proj_kernel(x0_ref, xap_ref, xas_ref, xbp_ref, xbs_ref, nw_ref, w_ref, wg_ref, bg_ref, o_ref, g_ref, xn_ref,
                    *, prompt_halves):
    g = pl.program_id(0)
    nw = nw_ref[0]

    def normalise(x_ref, slot, xs_ref=None, half=None):
        for r in range(IN_TH // IN_RC):
            rows = slice(r * IN_RC, (r + 1) * IN_RC)
            x = x_ref[0, rows, :]
            if xs_ref is not None:
                x = jnp.where(half < prompt_halves, x, xs_ref[0, rows, :])
            ms = jnp.mean(x * x, axis=-1, keepdims=True)
            xn_ref[slot, rows, :] = (x * lax.rsqrt(ms + EPS) * nw).astype(_bf16)

    def project(slot):
        rows = slice(slot * IN_TH, (slot + 1) * IN_TH)
        xp = xn_ref[slot]
        lr = _nt(xp, w_ref[0, N_MAIN:N_MAIN + 2 * GATE_RANK, :]).astype(_bf16)
        tiles = [(CB_ZA, CB_QB), (CB_ZB, N_CB), (CB_QA, CB_KA), (CB_QB, CB_VB), (CB_VB, CB_ZB), (CB_KA, CB_ZA)]
        for n, (cb0, cb1) in enumerate(tiles):
            acc = _nt(xp, w_ref[0, cb0 * LANE:cb1 * LANE, :])
            if _is_gate_block(cb0):
                acc = _silu(acc)
            for c in range(cb1 - cb0):
                o_ref[0, cb0 + c, rows, :] = acc[:, c * LANE:(c + 1) * LANE].astype(_bf16)
            if n == 0:
                gates = _log2_sigmoid_scaled(_nn(lr, wg_ref[0]) + bg_ref[0], 1.0 / GATE_NORM).astype(_bf16)
                for c in range(N_GATE_CB):
                    g_ref[0, c, rows, :] = gates[:, c * LANE:(c + 1) * LANE]

    @pl.when(g == 0)
    def _():
        normalise(x0_ref, 0)

    normalise(xap_ref, 1, xas_ref, 2 * g + 1)
    project(0)
    normalise(xbp_ref, 0, xbs_ref, 2 * g + 2)
    project(1)


def _in_proj(x_prompt, x_sample, norm_w, w_in, wg, bg, layer):
    bp, L, _ = x_prompt.shape
    bs = x_sample.shape[0]
    B = bp + bs
    nh = L // IN_TH
    hp, hs = bp * nh, bs * nh
    steps = (hp + hs) // 2

    def half_map(offset, first, count):
        def index(g):
            t = jnp.clip(2 * g + offset - first, 0, count - 1)
            return (t // nh, t % nh, 0)
        return index

    def out_map(g):
        return (g // (nh // 2), 0, g % (nh // 2), 0)

    half_spec = (1, IN_TH, D_MODEL)
    return pl.pallas_call(
        functools.partial(_in_proj_kernel, prompt_halves=hp),
        grid=(steps,),
        in_specs=[
            pl.BlockSpec(half_spec, lambda g: (0, 0, 0), pipeline_mode=pl.Buffered(1)),
            pl.BlockSpec(half_spec, half_map(1, 0, hp)),
            pl.BlockSpec(half_spec, half_map(1, hp, hs)),
            pl.BlockSpec(half_spec, half_map(2, 0, hp)),
            pl.BlockSpec(half_spec, half_map(2, hp, hs)),
            pl.BlockSpec((1, 1, D_MODEL), lambda g: (layer, 0, 0)),
            pl.BlockSpec((1, w_in.shape[1], D_MODEL), lambda g: (layer, 0, 0), pipeline_mode=pl.Buffered(1)),
            pl.BlockSpec((1, 2 * GATE_RANK, 2 * WBK), lambda g: (layer, 0, 0)),
            pl.BlockSpec((1, 1, 2 * WBK), lambda g: (layer, 0, 0)),
        ],
        out_specs=[
            pl.BlockSpec((1, N_CB, 2 * IN_TH, LANE), out_map),
            pl.BlockSpec((1, N_GATE_CB, 2 * IN_TH, LANE), out_map),
        ],
        out_shape=[
            jax.ShapeDtypeStruct((B, N_CB, L, LANE), _bf16),
            jax.ShapeDtypeStruct((B, N_GATE_CB, L, LANE), _bf16),
        ],
        scratch_shapes=[pltpu.VMEM((2, IN_TH, D_MODEL), _bf16)],
        compiler_params=pltpu.CompilerParams(
            dimension_semantics=("arbitrary",),
            vmem_limit_bytes=VMEM_LIMIT),
        name="in_proj",
    )(x_prompt, x_prompt, x_sample, x_prompt, x_sample, norm_w, w_in, wg, bg)


def _attn_kernel(sink_ref, q_ref, k_ref, v_ref, z_ref, bias_ref, o_ref, s_ref, p_ref, t_ref, *, nblk):
    h = pl.program_id(1)

    def blk_rows(j):
        return slice(j * BLOCK, (j + 1) * BLOCK)

    def window(ref, blk):
        lb = max(blk - 1, 0)
        rb = min(blk + 1, nblk - 1)
        return jnp.concatenate([ref[0, 0, blk_rows(lb), :], ref[0, 0, blk_rows(blk), :],
                                ref[0, 0, blk_rows(rb), :]], axis=0)

    def stage_a(blk, slot):
        q4 = q_ref[0, :, blk_rows(blk), :].reshape(GROUP * BLOCK, DH)
        s_ref[slot] = _nt(q4, window(k_ref, blk))

    def stage_b(blk, src, dst):
        table = 1 if blk == 0 else (2 if blk == nblk - 1 else 0)
        for g in range(GROUP):
            gr = slice(g * BLOCK, (g + 1) * BLOCK)
            sg = s_ref[src, gr, :] + bias_ref[table, g]
            sk = sink_ref[h * GROUP + g] * LOG2E
            m = jnp.maximum(jnp.max(sg, axis=-1, keepdims=True), sk)
            p_ref[dst, gr, :] = jnp.exp2(sg - m).astype(_bf16)
            t_ref[dst, g] = jnp.broadcast_to(jnp.exp2(sk - m), (BLOCK, DH))

    def stage_c(blk, src):
        vw = window(v_ref, blk)
        vw = jnp.concatenate([vw, jnp.ones_like(vw)], axis=-1)
        o4 = _nn(p_ref[src], vw)
        rows = blk_rows(blk)
        for g in range(GROUP):
            og = o4[g * BLOCK:(g + 1) * BLOCK]
            z = z_ref[0, g, rows, :].astype(_f32)
            den = og[:, DH:] + t_ref[src, g]
            o_ref[0, g, rows, :] = (og[:, :DH] * (1.0 / den) * z).astype(_bf16)

    for it in range(nblk + 2):
        par = it % 2
        if 0 <= it - 2 < nblk:
            stage_c(it - 2, 1 - par)
        if 0 <= it - 1 < nblk:
            stage_b(it - 1, 1 - par, par)
        if it < nblk:
            stage_a(it, par)


def _attention(proj, pos_bias, sink):
    B, _, L, _ = proj.shape
    nblk = L // BLOCK
    assert nblk >= 2 and nblk % 2 == 0
    grid = (B, HKV)
    return pl.pallas_call(
        functools.partial(_attn_kernel, nblk=nblk),
        grid=grid,
        in_specs=[
            pl.BlockSpec(memory_space=pltpu.SMEM),
            pl.BlockSpec((1, GROUP, L, LANE), lambda b, h: (b, CB_QA // GROUP + h, 0, 0)),
            pl.BlockSpec((1, 1, L, LANE), lambda b, h: (b, CB_KA + h, 0, 0)),
            pl.BlockSpec((1, 1, L, LANE), lambda b, h: (b, CB_VA + h, 0, 0)),
            pl.BlockSpec((1, GROUP, L, LANE), lambda b, h: (b, CB_ZA // GROUP + h, 0, 0)),
            pl.BlockSpec((3, GROUP, BLOCK, 3 * BLOCK), lambda b, h: (0, h, 0, 0)),
        ],
        out_specs=pl.BlockSpec((1, GROUP, L, LANE), lambda b, h: (b, h, 0, 0)),
        out_shape=jax.ShapeDtypeStruct((B, HA, L, LANE), _bf16),
        scratch_shapes=[
            pltpu.VMEM((2, GROUP * BLOCK, 3 * BLOCK), _f32),
            pltpu.VMEM((2, GROUP * BLOCK, 3 * BLOCK), _bf16),
            pltpu.VMEM((2, GROUP, BLOCK, DH), _f32),
        ],
        compiler_params=pltpu.CompilerParams(
            dimension_semantics=("parallel", "parallel"),
            vmem_limit_bytes=VMEM_LIMIT),
        name="attention",
    )(sink, proj, proj, proj, proj, pos_bias)


GLA_NH = 2
GLA_NT = 2


def _gla_kernel(q_ref, k_ref, v_ref, z_ref, gf_ref, gb_ref, gn_ref,
                o_ref,
                qd_ref, kk_ref, qs_ref, ks_ref, gdec_ref, vt_ref, pre_ref, st_ref, oacc_ref, *, ngroups):
    R = GROUP_ROWS
    row = lax.broadcasted_iota(jnp.int32, (R, R), 0)
    colm = lax.broadcasted_iota(jnp.int32, (R, R), 1)
    same_chunk = (row >= CHUNK) == (colm >= CHUNK)
    tri = jnp.where(same_chunk & (colm <= row), 1.0, 0.0).astype(_bf16)
    mask_d = (same_chunk & (colm <= row), same_chunk & (colm >= row))
    mask_o = ((row >= CHUNK) & (colm < CHUNK), (row < CHUNK) & (colm >= CHUNK))
    g_refs = (gf_ref, gb_ref)

    def grp_rows(g):
        return slice(g * R, (g + 1) * R)

    def log_gates(d, rows):
        return jnp.concatenate([g_refs[d][0, hh, rows, :] for hh in range(GLA_NH)], axis=-1)

    def rep(x):
        return jnp.broadcast_to(x, (CHUNK, x.shape[-1]))

    def prefix(d, g, slot):
        pre = _nn(tri, log_gates(d, grp_rows(g)))
        yield
        yield
        pre_ref[slot, d] = pre

    def operands(d, g, slot):
        yield
        rows = grp_rows(g)
        pre = pre_ref[slot, d]
        t0, t1 = pre[CHUNK - 1:CHUNK], pre[R - 1:R]
        e0, e1 = jnp.exp2(t0), jnp.exp2(t1)
        tot = jnp.concatenate([rep(t0), rep(t1)], axis=0)
        b = pre if d == 0 else tot - pre + log_gates(d, rows).astype(_f32)
        eb, enb = jnp.exp2(b), jnp.exp2(-b)
        etot = jnp.concatenate([rep(e0), rep(e1)], axis=0)
        for hh in range(GLA_NH):
            sl = slice(hh * DK, (hh + 1) * DK)
            q = q_ref[0, hh, rows, :].astype(_f32)
            k = k_ref[0, hh, rows, :].astype(_f32)
            qd = q * eb[:, sl]
            kd = k * enb[:, sl]
            kt = kd * etot[:, sl]
            if d == 0:
                qs = jnp.concatenate([qd[:CHUNK], qd[CHUNK:] * e0[:, sl]], axis=0)
                ks = jnp.concatenate([kt[:CHUNK] * e1[:, sl], kt[CHUNK:]], axis=0)
            else:
                qs = jnp.concatenate([qd[:CHUNK] * e1[:, sl], qd[CHUNK:]], axis=0)
                ks = jnp.concatenate([kt[:CHUNK], kt[CHUNK:] * e0[:, sl]], axis=0)
            qd_ref[slot, hh, d] = qd.astype(_bf16)
            kk_ref[slot, hh, d, 0:R, :] = kd.astype(_bf16)
            kk_ref[slot, hh, d, R:2 * R, :] = kt.astype(_bf16)
            qs_ref[slot, hh, d] = qs.astype(_bf16)
            ks_ref[slot, hh, d] = ks.astype(_bf16)
            gdec_ref[slot, hh, d] = jnp.broadcast_to((e0 * e1)[:, sl], (8, DK))
            v2 = jnp.concatenate([v_ref[0, 2 * hh, rows, :], v_ref[0, 2 * hh + 1, rows, :]], axis=-1)
            vt_ref[slot, hh, d] = v2.T

    gn = gn_ref[0]

    def finish(hh, rows, o):
        o = o + oacc_ref[hh, rows, :]
        ms = jnp.mean(o * o, axis=-1, keepdims=True)
        y = o * lax.rsqrt(ms + EPS) * gn
        z = jnp.concatenate([z_ref[0, 2 * hh, rows, :], z_ref[0, 2 * hh + 1, rows, :]], axis=-1).astype(_f32)
        y = (y * z).astype(_bf16)
        o_ref[0, 2 * hh, rows, :] = y[:, :LANE]
        o_ref[0, 2 * hh + 1, rows, :] = y[:, LANE:]

    def recurrence(d, hh, trips, second_visit):
        firsts = []
        for t, slot in trips:
            rows = grp_rows(t if d == 0 else ngroups - 1 - t)
            v2 = jnp.concatenate([v_ref[0, 2 * hh, rows, :], v_ref[0, 2 * hh + 1, rows, :]], axis=-1)
            p = _nt(qd_ref[slot, hh, d], kk_ref[slot, hh, d])
            up = _nn(vt_ref[slot, hh, d], ks_ref[slot, hh, d])
            firsts.append((rows, v2, p, up, qs_ref[slot, hh, d], gdec_ref[slot, hh, d][0:1]))
        yield
        st = st_ref[hh, d]
        outs = []
        for rows, v2, p, up, qs, gdec in firsts:
            a = jnp.where(mask_d[d], p[:, :R], jnp.where(mask_o[d], p[:, R:], 0.0)).astype(_bf16)
            outs.append((rows, _nn(a, v2) + _nt(qs, st.astype(_bf16))))
            st = st * gdec + up
        st_ref[hh, d] = st
        yield
        for rows, o in outs:
            if second_visit:
                finish(hh, rows, o)
            else:
                oacc_ref[hh, rows, :] = o

    def run_staged(tasks):
        tasks = list(tasks)
        while tasks:
            alive = []
            for task in tasks:
                try:
                    next(task)
                    alive.append(task)
                except StopIteration:
                    pass
            tasks = alive

    def per_trip(fn, pair_idx):
        tasks = []
        for slot in range(GLA_NT):
            t = GLA_NT * pair_idx + slot
            if t < ngroups:
                tasks += [fn(0, t, slot), fn(1, ngroups - 1 - t, slot)]
        return tasks

    st_ref[...] = jnp.zeros_like(st_ref)
    run_staged(per_trip(prefix, 0))
    run_staged(per_trip(operands, 0))
    run_staged(per_trip(prefix, 1))

    for tt in range(ngroups // GLA_NT):
        second_visit = tt >= ngroups // (2 * GLA_NT)
        trips = [(GLA_NT * tt + s, s) for s in range(GLA_NT)]
        rec = [recurrence(d, hh, trips, second_visit) for d in range(2) for hh in range(GLA_NH)]
        run_staged(per_trip(prefix, tt + 2) + rec + per_trip(operands, tt + 1))


def _gla(proj, gates, gla_norm, layer):
    B, _, L, _ = proj.shape
    ngroups = L // GROUP_ROWS
    assert ngroups % (2 * GLA_NT) == 0 and HB % GLA_NH == 0
    nh = GLA_NH
    grid = (B, HB // nh)
    slots = (GLA_NT, nh, 2)
    return pl.pallas_call(
        functools.partial(_gla_kernel, ngroups=ngroups),
        grid=grid,
        in_specs=[
            pl.BlockSpec((1, nh, L, LANE), lambda b, h: (b, CB_QB // nh + h, 0, 0)),
            pl.BlockSpec((1, nh, L, LANE), lambda b, h: (b, CB_KB // nh + h, 0, 0)),
            pl.BlockSpec((1, 2 * nh, L, LANE), lambda b, h: (b, CB_VB // (2 * nh) + h, 0, 0)),
            pl.BlockSpec((1, 2 * nh, L, LANE), lambda b, h: (b, CB_ZB // (2 * nh) + h, 0, 0)),
            pl.BlockSpec((1, nh, L, LANE), lambda b, h: (b, h, 0, 0)),
            pl.BlockSpec((1, nh, L, LANE), lambda b, h: (b, HB // nh + h, 0, 0)),
            pl.BlockSpec((1, 1, DV), lambda b, h: (layer, 0, 0)),
        ],
        out_specs=pl.BlockSpec((1, 2 * nh, L, LANE), lambda b, h: (b, h, 0, 0)),
        out_shape=jax.ShapeDtypeStruct((B, 2 * HB, L, LANE), _bf16),
        scratch_shapes=[
            pltpu.VMEM(slots + (GROUP_ROWS, DK), _bf16),
            pltpu.VMEM(slots + (2 * GROUP_ROWS, DK), _bf16),
            pltpu.VMEM(slots + (GROUP_ROWS, DK), _bf16),
            pltpu.VMEM(slots + (GROUP_ROWS, DK), _bf16),
            pltpu.VMEM(slots + (8, DK), _f32),
            pltpu.VMEM(slots + (DV, GROUP_ROWS), _bf16),
            pltpu.VMEM((GLA_NT, 2, GROUP_ROWS, nh * DK), _f32),
            pltpu.VMEM((nh, 2, DV, DK), _f32),
            pltpu.VMEM((nh, L, DV), _f32),
        ],
        compiler_params=pltpu.CompilerParams(
            dimension_semantics=("parallel", "parallel"),
            vmem_limit_bytes=VMEM_LIMIT),
        name="gla",
    )(proj, proj, proj, proj, gates, gates, gla_norm)


OUT_TM = 1024


OUT_TH = 512


def _out_proj_kernel(a_ref, g_ref, w_ref, x_ref, nw_ref, o_ref):
    nw = nw_ref[0]
    for r in range(OUT_TM // OUT_TH):
        rows = slice(r * OUT_TH, (r + 1) * OUT_TH)
        mix = jnp.concatenate([a_ref[0, c, rows, :] for c in range(HA)]
                              + [g_ref[0, c, rows, :] for c in range(2 * HB)], axis=-1)
        y = _nn(mix, w_ref[0])
        ms = jnp.mean(y * y, axis=-1, keepdims=True)
        o_ref[0, rows, :] = x_ref[0, rows, :] + y * lax.rsqrt(ms + EPS) * nw


def _out_proj(attn, gla, w_out, x, norm_w, layer, row0):
    B, L, _ = x.shape
    grid = (B, L // OUT_TM)
    return pl.pallas_call(
        _out_proj_kernel,
        grid=grid,
        in_specs=[
            pl.BlockSpec((1, HA, OUT_TM, LANE), lambda b, i: (row0 + b, 0, i, 0)),
            pl.BlockSpec((1, 2 * HB, OUT_TM, LANE), lambda b, i: (row0 + b, 0, i, 0)),
            pl.BlockSpec((1, D_MODEL, D_MODEL), lambda b, i: (layer, 0, 0), pipeline_mode=pl.Buffered(1)),
            pl.BlockSpec((1, OUT_TM, D_MODEL), lambda b, i: (b, i, 0)),
            pl.BlockSpec((1, 1, D_MODEL), lambda b, i: (layer, 0, 0)),
        ],
        out_specs=pl.BlockSpec((1, OUT_TM, D_MODEL), lambda b, i: (b, i, 0)),
        out_shape=jax.ShapeDtypeStruct((B, L, D_MODEL), _f32),
        compiler_params=pltpu.CompilerParams(
            dimension_semantics=("parallel", "parallel"),
            vmem_limit_bytes=VMEM_LIMIT),
        name="out_proj",
    )(attn, gla, w_out, x, norm_w)


def _band_tables():
    qi = np.arange(BLOCK)[:, None]
    kj = np.arange(3 * BLOCK)[None, :]
    rel = kj - BLOCK - qi
    nb = N_BUCKETS // 2
    max_exact = nb // 2
    n = np.abs(rel)
    large = max_exact + (np.log(np.maximum(n, 1) / max_exact) / np.log(MAX_DIST / max_exact)
                         * (nb - max_exact)).astype(np.int32)
    large = np.minimum(large, nb - 1)
    bucket = (rel > 0).astype(np.int32) * nb + np.where(n < max_exact, n, large)
    return bucket.astype(np.int32), (n <= WINDOW)


def _position_bias(rel_bias):
    bucket, in_band = _band_tables()
    onehot = jnp.asarray(np.eye(N_BUCKETS, dtype=np.float32)[bucket])
    pos_bias = jnp.einsum("qkn,nh->hqk", onehot, rel_bias.astype(_f32), precision=lax.Precision.HIGHEST)
    key_block = np.arange(3 * BLOCK)[None, :] // BLOCK
    tables = [jnp.where((in_band & keep)[None], pos_bias * LOG2E, NEG)
              for keep in (key_block >= 0, key_block >= 1, key_block <= 1)]
    return jnp.stack(tables, axis=0)


def _gate_weights(w_gk_fwd, w_gk_bwd):
    zeros = jnp.zeros_like(w_gk_fwd)
    return jnp.concatenate([jnp.concatenate([w_gk_fwd, zeros], axis=1),
                            jnp.concatenate([zeros, w_gk_bwd], axis=1)], axis=2).astype(_bf16)


def _layer(x_prompt, x_sample, layer, pos_bias, w_in, wg, bg, sink, gla_norm, w_out, norm_pre, norm_post):
    proj, gates = _in_proj(x_prompt, x_sample, norm_pre, w_in, wg, bg, layer)
    attn = _attention(proj, pos_bias, sink[layer])
    gla = _gla(proj, gates, gla_norm, layer)
    y_prompt = _out_proj(attn, gla, w_out, x_prompt, norm_post, layer, 0)
    y_sample = _out_proj(attn, gla, w_out, x_sample, norm_post, layer, x_prompt.shape[0])
    return y_prompt, y_sample


def _prepare(rel_bias, w_in, w_gk_fwd, b_gk_fwd, w_gk_bwd, b_gk_bwd, sink, gla_norm, w_out, norm_pre, norm_post):
    wg = _gate_weights(w_gk_fwd, w_gk_bwd)
    bg = jnp.concatenate([b_gk_fwd, b_gk_bwd], axis=-1)[:, None, :]
    col_scale = np.ones((w_in.shape[2],), np.float32)
    col_scale[CB_QA * LANE:CB_QA * LANE + WA] = DH ** -0.5 * LOG2E
    col_scale[CB_QB * LANE:CB_QB * LANE + WBK] = DK ** -0.5
    w_in = jnp.swapaxes(w_in * col_scale, 1, 2).astype(_bf16)
    return (_position_bias(rel_bias), w_in, wg, bg,
            sink, gla_norm[:, None, :], w_out.astype(_bf16), norm_pre[:, None, :], norm_post[:, None, :])


def kernel(x_prompt, x_sample, rel_bias, w_in, w_gk_fwd, b_gk_fwd, w_gk_bwd, b_gk_bwd, sink, gla_norm, w_out, norm_pre, norm_post):
    params = _prepare(rel_bias, w_in, w_gk_fwd, b_gk_fwd, w_gk_bwd, b_gk_bwd, sink, gla_norm, w_out,
                      norm_pre, norm_post)
    y_prompt, y_sample = x_prompt, x_sample
    for l in range(w_in.shape[0]):
        y_prompt, y_sample = _layer(y_prompt, y_sample, l, *params)
    return (y_prompt, y_sample)
```
